```python
import jax, jax.numpy as jnp
from jax import lax
import numpy as np

D_MODEL = 2048
BATCH = 8
SEQ = 2048
DEPTH = 1

MEM_LEN = 256
EPS = 1e-6
CHUNK = 128
A_GROUPS = 4
A_GROUP_CH = 128
A_WIDTH = A_GROUPS * A_GROUP_CH
WINDOW = 128
B_HEADS = 16
B_KV_HEADS = 2
B_HEAD_DIM = 64
B_WIDTH = B_HEADS * B_HEAD_DIM
B_KV_WIDTH = B_KV_HEADS * B_HEAD_DIM
ROPE_DIM = B_HEAD_DIM // 4
ROPE_THETA = 500000.0
C_HEADS = 4
C_HEAD_DIM = 128
C_WIDTH = C_HEADS * C_HEAD_DIM
N_BRANCH = 3
D_FF = 5632
CONV_W = 3

SPLITS = list(np.cumsum([A_WIDTH, A_WIDTH, B_WIDTH, B_KV_WIDTH, B_KV_WIDTH, C_WIDTH]).tolist())
IN_COLS = 2 * A_WIDTH + B_WIDTH + 2 * B_KV_WIDTH + C_WIDTH + N_BRANCH * D_MODEL

kernel_name = "hybrid_gated_parallel_mixers"


def rmsnorm(x, g):
    xf = x.astype(jnp.float32)
    y = xf * lax.rsqrt(jnp.mean(xf * xf, axis=-1, keepdims=True) + EPS)
    return (y * g.astype(jnp.float32)).astype(x.dtype)


def partial_rope(x, pos):
    half = ROPE_DIM // 2
    inv = ROPE_THETA ** (-jnp.arange(half, dtype=jnp.float32) / half)
    ang = pos.astype(jnp.float32)[..., None] * inv
    cos = jnp.cos(ang)[:, :, None, :]
    sin = jnp.sin(ang)[:, :, None, :]
    xr = x[..., :ROPE_DIM].astype(jnp.float32)
    x1, x2 = xr[..., :half], xr[..., half:]
    rot = jnp.concatenate([x1 * cos - x2 * sin, x2 * cos + x1 * sin], axis=-1)
    return jnp.concatenate([rot.astype(x.dtype), x[..., ROPE_DIM:]], axis=-1)


def chunked_spatial_gating(u, v, g_v, w_s, b_s):
    bn, s_len, _ = u.shape
    nc = s_len // CHUNK
    u = jax.nn.gelu(u)
    v = rmsnorm(jax.nn.gelu(v), g_v)
    v = v.reshape(bn, nc, CHUNK, A_GROUPS, A_GROUP_CH)
    causal = jnp.tril(jnp.ones((CHUNK, CHUNK), dtype=bool))
    w = jnp.where(causal[None], w_s, jnp.zeros_like(w_s))
    s = jnp.einsum('gts,bnsgc->bntgc', w, v) + b_s.T[None, None, :, :, None]
    return u * s.reshape(bn, s_len, A_WIDTH)


def sliding_window_gqa(q, k, v, g_q, g_k, sinks, pos):
    bn, s_len = q.shape[:2]
    nb = s_len // WINDOW
    rep = B_HEADS // B_KV_HEADS
    q = partial_rope(rmsnorm(q, g_q), pos)
    k = partial_rope(rmsnorm(k, g_k), pos)
    qb = q.reshape(bn, nb, WINDOW, B_KV_HEADS, rep, B_HEAD_DIM)
    kb = k.reshape(bn, nb, WINDOW, B_KV_HEADS, B_HEAD_DIM)
    vb = v.reshape(bn, nb, WINDOW, B_KV_HEADS, B_HEAD_DIM)
    pad_k = jnp.zeros_like(kb[:, :1])
    pad_v = jnp.zeros_like(vb[:, :1])
    k2 = jnp.concatenate([jnp.concatenate([pad_k, kb[:, :-1]], axis=1), kb], axis=2)
    v2 = jnp.concatenate([jnp.concatenate([pad_v, vb[:, :-1]], axis=1), vb], axis=2)
    s = jnp.einsum('bnqhrd,bnkhd->bnhrqk', qb, k2,
                   preferred_element_type=jnp.float32) * (B_HEAD_DIM ** -0.5)
    qi = jnp.arange(WINDOW)[:, None] + WINDOW
    kj = jnp.arange(2 * WINDOW)[None, :]
    rel = qi - kj
    band = (rel >= 0) & (rel < WINDOW)
    blk = jnp.arange(nb)[:, None, None]
    valid = band[None] & ((kj[None] >= WINDOW) | (blk > 0))
    s = jnp.where(valid[None, :, None, None], s, -jnp.inf)
    sink_col = jnp.broadcast_to(
        sinks.astype(jnp.float32).reshape(1, 1, B_KV_HEADS, rep, 1, 1), s.shape[:-1] + (1,))
    p = jax.nn.softmax(jnp.concatenate([s, sink_col], axis=-1), axis=-1)[..., :-1]
    o = jnp.einsum('bnhrqk,bnkhd->bnqhrd', p.astype(v.dtype), v2)
    return o.reshape(bn, s_len, B_WIDTH)


def memory_cross_attention(q, mem_h, w_mem_kv, g_q, g_k):
    bn, s_len = q.shape[:2]
    m_len = mem_h.shape[1]
    kv = (mem_h @ w_mem_kv).reshape(bn, m_len, 2, C_HEADS, C_HEAD_DIM)
    k, v = kv[:, :, 0], kv[:, :, 1]
    q = rmsnorm(q, g_q)
    k = rmsnorm(k, g_k)
    s = jnp.einsum('bshd,bmhd->bhsm', q, k,
                   preferred_element_type=jnp.float32) * (C_HEAD_DIM ** -0.5)
    p = jax.nn.softmax(s, axis=-1)
    o = jnp.einsum('bhsm,bmhd->bshd', p.astype(v.dtype), v)
    return o.reshape(bn, s_len, C_WIDTH)


def gated_conv_ffn(h, w_up, conv_w, conv_b, w_down):
    up = h @ w_up
    c = up.shape[-1]
    up = lax.conv_general_dilated(
        up, conv_w[:, None, :].astype(up.dtype), window_strides=(1,),
        padding=[(CONV_W - 1, 0)], dimension_numbers=('NWC', 'WIO', 'NWC'),
        feature_group_count=c) + conv_b
    a, b = up[..., :D_FF], up[..., D_FF:]
    return (jax.nn.silu(a) * b) @ w_down


def setup_inputs(seed: int = 0) -> dict:
    key = jax.random.key(seed)
    ks = jax.random.split(key, 26)
    f32 = jnp.float32
    L = DEPTH

    def nrm(k, shape, scale):
        return jax.random.normal(k, shape, f32) * scale

    def gain(k, n):
        return 1.0 + 0.02 * jax.random.normal(k, (L, n), f32)

    x = nrm(ks[0], (BATCH, SEQ, D_MODEL), 1.0)
    mem = nrm(ks[1], (BATCH, MEM_LEN, D_MODEL), 1.0)
    positions = (jax.random.randint(ks[2], (BATCH, 1), 0, 4096, jnp.int32)
                 + jnp.arange(SEQ, dtype=jnp.int32)[None, :])
    return {
        "x": x,
        "mem": mem,
        "positions": positions,
        "g_mix": gain(ks[3], D_MODEL),
        "w_in": nrm(ks[4], (L, D_MODEL, IN_COLS), D_MODEL ** -0.5),
        "g_a_v": gain(ks[5], A_WIDTH),
        "w_spatial": nrm(ks[6], (L, A_GROUPS, CHUNK, CHUNK), CHUNK ** -0.5),
        "b_spatial": 1.0 + 0.1 * jax.random.normal(ks[7], (L, A_GROUPS, CHUNK), f32),
        "g_b_q": gain(ks[8], B_HEAD_DIM),
        "g_b_k": gain(ks[9], B_HEAD_DIM),
        "sinks": nrm(ks[10], (L, B_HEADS), 0.5),
        "g_mem": gain(ks[11], D_MODEL),
        "w_mem_kv": nrm(ks[12], (L, D_MODEL, 2 * C_WIDTH), D_MODEL ** -0.5),
        "g_c_q": gain(ks[13], C_HEAD_DIM),
        "g_c_k": gain(ks[14], C_HEAD_DIM),
        "w_branch_a": nrm(ks[15], (L, A_WIDTH, D_MODEL), A_WIDTH ** -0.5),
        "w_branch_b": nrm(ks[16], (L, B_WIDTH, D_MODEL), B_WIDTH ** -0.5),
        "w_branch_c": nrm(ks[17], (L, C_WIDTH, D_MODEL), C_WIDTH ** -0.5),
        "w_out": nrm(ks[18], (L, D_MODEL, D_MODEL), D_MODEL ** -0.5),
        "g_ffn": gain(ks[19], D_MODEL),
        "w_up": nrm(ks[20], (L, D_MODEL, 2 * D_FF), D_MODEL ** -0.5),
        "conv_w": nrm(ks[21], (L, CONV_W, 2 * D_FF), CONV_W ** -0.5),
        "conv_b": nrm(ks[22], (L, 2 * D_FF), 0.01),
        "w_down": nrm(ks[23], (L, D_FF, D_MODEL), D_FF ** -0.5),
    }


def reference(x, mem, positions, g_mix, w_in, g_a_v, w_spatial, b_spatial, g_b_q, g_b_k,
              sinks, g_mem, w_mem_kv, g_c_q, g_c_k, w_branch_a, w_branch_b, w_branch_c,
              w_out, g_ffn, w_up, conv_w, conv_b, w_down):
    bn, s_len, _ = x.shape
    for l in range(DEPTH):
        h = rmsnorm(x, g_mix[l])
        proj = h @ w_in[l]
        u_a, v_a, q_b, k_b, v_b, q_c, gates = jnp.split(proj, SPLITS, axis=-1)
        y_a = chunked_spatial_gating(u_a, v_a, g_a_v[l], w_spatial[l], b_spatial[l])
        y_b = sliding_window_gqa(
            q_b.reshape(bn, s_len, B_HEADS, B_HEAD_DIM),
            k_b.reshape(bn, s_len, B_KV_HEADS, B_HEAD_DIM),
            v_b.reshape(bn, s_len, B_KV_HEADS, B_HEAD_DIM),
            g_b_q[l], g_b_k[l], sinks[l], positions)
        y_c = memory_cross_attention(
            q_c.reshape(bn, s_len, C_HEADS, C_HEAD_DIM), rmsnorm(mem, g_mem[l]),
            w_mem_kv[l], g_c_q[l], g_c_k[l])
        gate = jax.nn.sigmoid(gates.reshape(bn, s_len, N_BRANCH, D_MODEL))
        merged = (gate[:, :, 0] * (y_a @ w_branch_a[l])
                  + gate[:, :, 1] * (y_b @ w_branch_b[l])
                  + gate[:, :, 2] * (y_c @ w_branch_c[l]))
        x = x + merged @ w_out[l]
        x = x + gated_conv_ffn(rmsnorm(x, g_ffn[l]), w_up[l], conv_w[l], conv_b[l], w_down[l])
    return x
```

```python
import functools

import jax
import jax.numpy as jnp
import numpy as np
from jax import lax
from jax.experimental import pallas as pl
from jax.experimental.pallas import tpu as pltpu

F32 = jnp.float32
BF16 = jnp.bfloat16

D_MODEL = 2048
MEM_LEN = 256
EPS = 1e-6
CHUNK = 128
A_GROUPS = 4
A_GROUP_CH = 128
A_WIDTH = A_GROUPS * A_GROUP_CH
WINDOW = 128
B_HEADS = 16
B_KV_HEADS = 2
B_REP = B_HEADS // B_KV_HEADS
B_HEAD_DIM = 64
B_WIDTH = B_HEADS * B_HEAD_DIM
B_KV_WIDTH = B_KV_HEADS * B_HEAD_DIM
ROPE_DIM = B_HEAD_DIM // 4
ROPE_HALF = ROPE_DIM // 2
ROPE_THETA = 500000.0
C_HEADS = 4
C_HEAD_DIM = 128
C_WIDTH = C_HEADS * C_HEAD_DIM
N_BRANCH = 3
D_FF = 5632
CONV_W = 3
MAIN_COLS = 2 * A_WIDTH + B_WIDTH + 2 * B_KV_WIDTH + C_WIDTH
IN_COLS = MAIN_COLS + N_BRANCH * D_MODEL

LANES = 128
SUBLANES = 8
VMEM_LIMIT_BYTES = 56 * 1024 * 1024

IN_TM = 1024
IN_TN = 1280
MIX_TM = 256
FFN_TM = 512
FFN_FC = 512

COL_QB = N_BRANCH * D_MODEL
COL_UV = COL_QB + B_WIDTH
COL_QC = COL_UV + 2 * A_WIDTH
COL_KV = COL_QC + C_WIDTH


def _rms(x, g):
    ms = jnp.mean(x * x, axis=-1, keepdims=True)
    return x * lax.rsqrt(ms + EPS) * g


def _mem_kv_kernel(mem_ref, g_ref, w_ref, gk_ref, kt_ref, v_ref):
    mh = _rms(mem_ref[0], g_ref[...]).astype(BF16)
    kv = jnp.dot(mh, w_ref[...], preferred_element_type=F32)
    for h in range(C_HEADS):
        k = _rms(kv[:, h * C_HEAD_DIM:(h + 1) * C_HEAD_DIM], gk_ref[...])
        kt_ref[0, h * C_HEAD_DIM:(h + 1) * C_HEAD_DIM, :] = k.T.astype(BF16)
    v_ref[0] = kv[:, C_WIDTH:].astype(BF16)


def _mem_kv(mem, g_mem, w_kv, g_c_k):
    bn = mem.shape[0]
    return pl.pallas_call(
        _mem_kv_kernel,
        grid=(bn,),
        in_specs=[
            pl.BlockSpec((1, MEM_LEN, D_MODEL), lambda b: (b, 0, 0)),
            pl.BlockSpec((1, D_MODEL), lambda b: (0, 0)),
            pl.BlockSpec((D_MODEL, 2 * C_WIDTH), lambda b: (0, 0)),
            pl.BlockSpec((1, C_HEAD_DIM), lambda b: (0, 0)),
        ],
        out_specs=[
            pl.BlockSpec((1, C_WIDTH, MEM_LEN), lambda b: (b, 0, 0)),
            pl.BlockSpec((1, MEM_LEN, C_WIDTH), lambda b: (b, 0, 0)),
        ],
        out_shape=[
            jax.ShapeDtypeStruct((bn, C_WIDTH, MEM_LEN), BF16),
            jax.ShapeDtypeStruct((bn, MEM_LEN, C_WIDTH), BF16),
        ],
        compiler_params=pltpu.CompilerParams(
            dimension_semantics=("arbitrary",), vmem_limit_bytes=VMEM_LIMIT_BYTES),
        name="mem_kv",
    )(mem, g_mem, w_kv, g_c_k)


def _in_proj_kernel(x_ref, g_ref, w_ref, o_ref, h_ref):
    @pl.when(pl.program_id(1) == 0)
    def _():
        h_ref[...] = _rms(x_ref[...], g_ref[...]).astype(BF16)

    o_ref[...] = jnp.dot(h_ref[...], w_ref[...],
                         preferred_element_type=F32).astype(o_ref.dtype)


def _in_proj(x2, g_mix, w_in_r):
    t = x2.shape[0]
    return pl.pallas_call(
        _in_proj_kernel,
        grid=(t // IN_TM, IN_COLS // IN_TN),
        in_specs=[
            pl.BlockSpec((IN_TM, D_MODEL), lambda i, j: (i, 0)),
            pl.BlockSpec((1, D_MODEL), lambda i, j: (0, 0)),
            pl.BlockSpec((D_MODEL, IN_TN), lambda i, j: (0, j)),
        ],
        out_specs=pl.BlockSpec((IN_TM, IN_TN), lambda i, j: (i, j)),
        out_shape=jax.ShapeDtypeStruct((t, IN_COLS), BF16),
        scratch_shapes=[pltpu.VMEM((IN_TM, D_MODEL), BF16)],
        compiler_params=pltpu.CompilerParams(
            dimension_semantics=("arbitrary", "arbitrary"),
            vmem_limit_bytes=VMEM_LIMIT_BYTES),
        name="in_proj",
    )(x2, g_mix, w_in_r)


def _rope_tables(pos, invf):
    ang = pos * invf
    cos = jnp.cos(ang)
    sin = jnp.sin(ang)
    lane = lax.broadcasted_iota(jnp.int32, ang.shape, 1) % B_HEAD_DIM
    s_lo = jnp.where(lane < ROPE_HALF, -sin, 0.0)
    s_hi = jnp.where(lane >= ROPE_HALF, sin, 0.0)
    return cos, s_lo, s_hi


def _rope(z, tabs):
    cos, s_lo, s_hi = tabs
    return (z * cos + pltpu.roll(z, LANES - ROPE_HALF, 1) * s_lo
            + pltpu.roll(z, ROPE_HALF, 1) * s_hi)


def _head_norm(z, bd, g):
    ssq = jnp.dot((z * z).astype(BF16), bd, preferred_element_type=F32)
    return z * lax.rsqrt(ssq * (1.0 / B_HEAD_DIM) + EPS) * g


def _mixer_kernel(sinks_ref, x_ref, g0_ref, g1_ref, g2_ref, qb_ref, uv_ref, qc_ref,
                  kv_ref, kvh_ref, pos_ref, posh_ref, kct_ref, vc_ref,
                  gav_ref, wsp_ref, bsp_ref, gq_ref, gk_ref, invf_ref, gcq_ref, bd_ref,
                  wa_ref, wb_ref, wc_ref, wo_ref, o_ref, y_ref):
    tm = x_ref.shape[0]
    n_chunks = tm // CHUNK
    tile = pl.program_id(1)
    bd = bd_ref[...]
    bd_kv = bd[:B_KV_WIDTH, :B_KV_WIDTH]

    uv = uv_ref[...].astype(F32)
    u = jax.nn.gelu(uv[:, :A_WIDTH])
    v = _rms(jax.nn.gelu(uv[:, A_WIDTH:]), gav_ref[...]).astype(BF16)
    row = lax.broadcasted_iota(jnp.int32, (CHUNK, CHUNK), 0)
    col = lax.broadcasted_iota(jnp.int32, (CHUNK, CHUNK), 1)
    w_sp = [jnp.where(row >= col, wsp_ref[g], 0.0).astype(BF16) for g in range(A_GROUPS)]
    for c in range(n_chunks):
        rows = slice(c * CHUNK, (c + 1) * CHUNK)
        s = jnp.concatenate(
            [jnp.dot(w_sp[g], v[rows, g * A_GROUP_CH:(g + 1) * A_GROUP_CH],
                     preferred_element_type=F32) for g in range(A_GROUPS)], axis=1)
        y_ref[rows, 0:A_WIDTH] = (u[rows] * (s + bsp_ref[...])).astype(BF16)

    invf = invf_ref[...]
    tabs = _rope_tables(pos_ref[...], invf)
    tabs_h = _rope_tables(posh_ref[...], invf)
    kv = kv_ref[...].astype(F32)
    kvh = kvh_ref[...].astype(F32)
    k_cur = _rope(_head_norm(kv[:, :B_KV_WIDTH], bd_kv, gk_ref[...]), tabs)
    k_prev = _rope(_head_norm(kvh[:, :B_KV_WIDTH], bd_kv, gk_ref[...]), tabs_h)
    kt_all = jnp.concatenate([k_prev, k_cur], axis=0).T.astype(BF16)
    v_all = jnp.concatenate([kvh_ref[:, B_KV_WIDTH:], kv_ref[:, B_KV_WIDTH:]], axis=0)

    qb = qb_ref[...].astype(F32)
    scale = B_HEAD_DIM ** -0.5
    q_slabs = []
    for j in range(B_WIDTH // 256):
        z = _head_norm(qb[:, j * 256:(j + 1) * 256], bd, gq_ref[:, j * 256:(j + 1) * 256])
        for half in range(2):
            zz = _rope(z[:, half * LANES:(half + 1) * LANES], tabs) * scale
            q_slabs.append(zz.astype(BF16))

    qi = lax.broadcasted_iota(jnp.int32, (WINDOW, 2 * WINDOW), 0) + WINDOW
    kj = lax.broadcasted_iota(jnp.int32, (WINDOW, 2 * WINDOW), 1)
    rel = qi - kj
    band = (rel >= 0) & (rel < WINDOW)
    for c in range(n_chunks):
        rows = slice(c * CHUNK, (c + 1) * CHUNK)
        if c == 0:
            valid = band & ((kj >= WINDOW) | (tile > 0))
        else:
            valid = band
        for h in range(B_KV_HEADS):
            kt = kt_all[h * B_HEAD_DIM:(h + 1) * B_HEAD_DIM, c * CHUNK:c * CHUNK + 2 * WINDOW]
            v2 = v_all[c * CHUNK:c * CHUNK + 2 * WINDOW, h * B_HEAD_DIM:(h + 1) * B_HEAD_DIM]
            q_stack = jnp.concatenate(
                [q_slabs[(h * B_REP + r) // 2][rows, (r % 2) * B_HEAD_DIM:(r % 2 + 1) * B_HEAD_DIM]
                 for r in range(B_REP)], axis=0)
            s_all = jnp.dot(q_stack, kt, preferred_element_type=F32)
            p_list, l_list = [], []
            for r in range(B_REP):
                s = jnp.where(valid, s_all[r * CHUNK:(r + 1) * CHUNK], -jnp.inf)
                sink = sinks_ref[h * B_REP + r]
                m = jnp.maximum(jnp.max(s, axis=-1, keepdims=True), sink)
                p = jnp.exp(s - m)
                l_list.append(jnp.sum(p, axis=-1, keepdims=True) + jnp.exp(sink - m))
                p_list.append(p.astype(BF16))
            o_all = jnp.dot(jnp.concatenate(p_list, axis=0), v2,
                            preferred_element_type=F32)
            for r in range(0, B_REP, 2):
                o0 = o_all[r * CHUNK:(r + 1) * CHUNK] / l_list[r]
                o1 = o_all[(r + 1) * CHUNK:(r + 2) * CHUNK] / l_list[r + 1]
                c0 = A_WIDTH + (h * B_REP + r) * B_HEAD_DIM
                y_ref[rows, c0:c0 + 2 * B_HEAD_DIM] = jnp.concatenate(
                    [o0, o1], axis=1).astype(BF16)

    qc = qc_ref[...].astype(F32)
    c_scale = C_HEAD_DIM ** -0.5
    for h in range(C_HEADS):
        cols = slice(h * C_HEAD_DIM, (h + 1) * C_HEAD_DIM)
        qn = _rms(qc[:, cols], gcq_ref[...]).astype(BF16)
        s = jnp.dot(qn, kct_ref[0, cols, :], preferred_element_type=F32) * c_scale
        m = jnp.max(s, axis=-1, keepdims=True)
        p = jnp.exp(s - m)
        l = jnp.sum(p, axis=-1, keepdims=True)
        o = jnp.dot(p.astype(BF16), vc_ref[0, :, cols], preferred_element_type=F32) / l
        c0 = A_WIDTH + B_WIDTH + h * C_HEAD_DIM
        y_ref[:, c0:c0 + C_HEAD_DIM] = o.astype(BF16)

    za = jnp.dot(y_ref[:, 0:A_WIDTH], wa_ref[...], preferred_element_type=F32)
    merged = jax.nn.sigmoid(g0_ref[...].astype(F32)) * za
    zb = jnp.dot(y_ref[:, A_WIDTH:A_WIDTH + B_WIDTH], wb_ref[...], preferred_element_type=F32)
    merged = merged + jax.nn.sigmoid(g1_ref[...].astype(F32)) * zb
    zc = jnp.dot(y_ref[:, A_WIDTH + B_WIDTH:], wc_ref[...], preferred_element_type=F32)
    merged = merged + jax.nn.sigmoid(g2_ref[...].astype(F32)) * zc
    o_ref[...] = x_ref[...] + jnp.dot(merged.astype(BF16), wo_ref[...],
                                      preferred_element_type=F32)


def _mixers(x2, proj, pos2, kct, vc, sinks, g_a_v, w_sp, bias_sp, gq_t, gk_t, invf, gcq,
            bd, wa, wb, wc, wo, bn, s_len):
    t = x2.shape[0]
    tm = MIX_TM
    tps = s_len // tm
    cpt = tm // CHUNK

    def row(b, i):
        return b * tps + i

    def halo(b, i):
        return jnp.maximum(row(b, i) * cpt - 1, 0)

    def const(*idx):
        return lambda b, i: idx

    in_specs = [
        pl.BlockSpec(memory_space=pltpu.SMEM),
        pl.BlockSpec((tm, D_MODEL), lambda b, i: (row(b, i), 0)),
        pl.BlockSpec((tm, D_MODEL), lambda b, i: (row(b, i), 0)),
        pl.BlockSpec((tm, D_MODEL), lambda b, i: (row(b, i), 1)),
        pl.BlockSpec((tm, D_MODEL), lambda b, i: (row(b, i), 2)),
        pl.BlockSpec((tm, B_WIDTH), lambda b, i: (row(b, i), COL_QB // B_WIDTH)),
        pl.BlockSpec((tm, 2 * A_WIDTH), lambda b, i: (row(b, i), COL_UV // (2 * A_WIDTH))),
        pl.BlockSpec((tm, C_WIDTH), lambda b, i: (row(b, i), COL_QC // C_WIDTH)),
        pl.BlockSpec((tm, 2 * B_KV_WIDTH), lambda b, i: (row(b, i), COL_KV // (2 * B_KV_WIDTH))),
        pl.BlockSpec((WINDOW, 2 * B_KV_WIDTH), lambda b, i: (halo(b, i), COL_KV // (2 * B_KV_WIDTH))),
        pl.BlockSpec((tm, 1), lambda b, i: (row(b, i), 0)),
        pl.BlockSpec((WINDOW, 1), lambda b, i: (halo(b, i), 0)),
        pl.BlockSpec((1, C_WIDTH, MEM_LEN), lambda b, i: (b, 0, 0)),
        pl.BlockSpec((1, MEM_LEN, C_WIDTH), lambda b, i: (b, 0, 0)),
        pl.BlockSpec((1, A_WIDTH), const(0, 0)),
        pl.BlockSpec((A_GROUPS, CHUNK, CHUNK), const(0, 0, 0)),
        pl.BlockSpec((CHUNK, A_WIDTH), const(0, 0)),
        pl.BlockSpec((1, B_WIDTH), const(0, 0)),
        pl.BlockSpec((1, B_KV_WIDTH), const(0, 0)),
        pl.BlockSpec((1, LANES), const(0, 0)),
        pl.BlockSpec((1, C_HEAD_DIM), const(0, 0)),
        pl.BlockSpec((256, 256), const(0, 0)),
        pl.BlockSpec((A_WIDTH, D_MODEL), const(0, 0), pipeline_mode=pl.Buffered(1)),
        pl.BlockSpec((B_WIDTH, D_MODEL), const(0, 0), pipeline_mode=pl.Buffered(1)),
        pl.BlockSpec((C_WIDTH, D_MODEL), const(0, 0), pipeline_mode=pl.Buffered(1)),
        pl.BlockSpec((D_MODEL, D_MODEL), const(0, 0), pipeline_mode=pl.Buffered(1)),
    ]
    return pl.pallas_call(
        _mixer_kernel,
        grid=(bn, tps),
        in_specs=in_specs,
        out_specs=pl.BlockSpec((tm, D_MODEL), lambda b, i: (row(b, i), 0)),
        out_shape=jax.ShapeDtypeStruct((t, D_MODEL), F32),
        scratch_shapes=[pltpu.VMEM((tm, D_MODEL), BF16)],
        compiler_params=pltpu.CompilerParams(
            dimension_semantics=("arbitrary", "arbitrary"),
            vmem_limit_bytes=VMEM_LIMIT_BYTES),
        name="mixers",
    )(sinks, x2, proj, proj, proj, proj, proj, proj, proj, proj, pos2, pos2, kct, vc,
      g_a_v, w_sp, bias_sp, gq_t, gk_t, invf, gcq, bd, wa, wb, wc, wo)


def _causal_conv(u, tail, cw, cb):
    r8 = lax.broadcasted_iota(jnp.int32, (SUBLANES, u.shape[1]), 0)
    outs = []
    for shift in (1, 2):
        r = pltpu.roll(u, shift, 0)
        top = jnp.where(r8 < shift, pltpu.roll(tail, shift, 0), r[:SUBLANES])
        outs.append(jnp.concatenate([top, r[SUBLANES:]], axis=0))
    return cw[0:1] * outs[1] + cw[1:2] * outs[0] + cw[2:3] * u + cb


def _ffn_kernel(tiles_per_seq, x_ref, g_ref, wua_ref, wub_ref, cwa_ref, cwb_ref,
                cba_ref, cbb_ref, wd_ref, o_ref, h_ref, tail_a_ref, tail_b_ref, acc_ref):
    i = pl.program_id(0)
    f = pl.program_id(1)
    tm = x_ref.shape[0]

    @pl.when(f == 0)
    def _():
        h_ref[...] = _rms(x_ref[...], g_ref[...]).astype(BF16)
        acc_ref[...] = jnp.zeros_like(acc_ref)

    @pl.when(i == 0)
    def _():
        tail_a_ref[f] = jnp.zeros(tail_a_ref.shape[1:], F32)
        tail_b_ref[f] = jnp.zeros(tail_b_ref.shape[1:], F32)

    h = h_ref[...]
    seq_start = i % tiles_per_seq == 0
    ua = jnp.dot(h, wua_ref[...], preferred_element_type=F32)
    ub = jnp.dot(h, wub_ref[...], preferred_element_type=F32)
    ca = _causal_conv(ua, jnp.where(seq_start, 0.0, tail_a_ref[f]), cwa_ref[...], cba_ref[...])
    cb = _causal_conv(ub, jnp.where(seq_start, 0.0, tail_b_ref[f]), cwb_ref[...], cbb_ref[...])
    tail_a_ref[f] = ua[tm - SUBLANES:]
    tail_b_ref[f] = ub[tm - SUBLANES:]
    act = (jax.nn.silu(ca) * cb).astype(BF16)
    acc_ref[...] += jnp.dot(act, wd_ref[...], preferred_element_type=F32)

    @pl.when(f == pl.num_programs(1) - 1)
    def _():
        o_ref[...] = x_ref[...] + acc_ref[...]


def _conv_ffn(x2, g_ffn, w_up, conv_w, conv_b, w_down, s_len):
    t = x2.shape[0]
    tm, fc = FFN_TM, FFN_FC
    nf = D_FF // fc
    return pl.pallas_call(
        functools.partial(_ffn_kernel, s_len // tm),
        grid=(t // tm, nf),
        in_specs=[
            pl.BlockSpec((tm, D_MODEL), lambda i, f: (i, 0)),
            pl.BlockSpec((1, D_MODEL), lambda i, f: (0, 0)),
            pl.BlockSpec((D_MODEL, fc), lambda i, f: (0, f)),
            pl.BlockSpec((D_MODEL, fc), lambda i, f: (0, nf + f)),
            pl.BlockSpec((CONV_W, fc), lambda i, f: (0, f)),
            pl.BlockSpec((CONV_W, fc), lambda i, f: (0, nf + f)),
            pl.BlockSpec((1, fc), lambda i, f: (0, f)),
            pl.BlockSpec((1, fc), lambda i, f: (0, nf + f)),
            pl.BlockSpec((fc, D_MODEL), lambda i, f: (f, 0)),
        ],
        out_specs=pl.BlockSpec((tm, D_MODEL), lambda i, f: (i, 0)),
        out_shape=jax.ShapeDtypeStruct((t, D_MODEL), F32),
        scratch_shapes=[
            pltpu.VMEM((tm, D_MODEL), BF16),
            pltpu.VMEM((nf, SUBLANES, fc), F32),
            pltpu.VMEM((nf, SUBLANES, fc), F32),
            pltpu.VMEM((tm, D_MODEL), F32),
        ],
        compiler_params=pltpu.CompilerParams(
            dimension_semantics=("arbitrary", "arbitrary"),
            vmem_limit_bytes=VMEM_LIMIT_BYTES),
        name="conv_ffn",
    )(x2, g_ffn, w_up, w_up, conv_w, conv_w, conv_b, conv_b, w_down)


def _reorder_in_cols(w_in):
    o = np.cumsum([0, A_WIDTH, A_WIDTH, B_WIDTH, B_KV_WIDTH, B_KV_WIDTH, C_WIDTH]).tolist()
    u_a, v_a, q_b, k_b, v_b, q_c = (w_in[:, o[n]:o[n + 1]] for n in range(6))
    return jnp.concatenate([w_in[:, MAIN_COLS:], q_b, u_a, v_a, q_c, k_b, v_b], axis=1)


def kernel(x, mem, positions, g_mix, w_in, g_a_v, w_spatial, b_spatial, g_b_q, g_b_k, sinks,
           g_mem, w_mem_kv, g_c_q, g_c_k, w_branch_a, w_branch_b, w_branch_c, w_out, g_ffn,
           w_up, conv_w, conv_b, w_down):
    bn, s_len, _ = x.shape
    depth = w_in.shape[0]
    assert s_len % MIX_TM == 0 and s_len % FFN_TM == 0 and (bn * s_len) % IN_TM == 0
    t = bn * s_len
    x2 = x.reshape(t, D_MODEL)
    pos2 = positions.astype(F32).reshape(t, 1)

    inv = ROPE_THETA ** (-jnp.arange(ROPE_HALF, dtype=F32) / ROPE_HALF)
    inv_head = jnp.concatenate([inv, inv, jnp.zeros((B_HEAD_DIM - ROPE_DIM,), F32)])
    invf = jnp.tile(inv_head, LANES // B_HEAD_DIM).reshape(1, LANES)
    grp = np.arange(256) // B_HEAD_DIM
    bd = jnp.asarray(grp[:, None] == grp[None, :], dtype=BF16)

    for l in range(depth):
        kct, vc = _mem_kv(mem, g_mem[l].reshape(1, D_MODEL), w_mem_kv[l].astype(BF16),
                          g_c_k[l].reshape(1, C_HEAD_DIM))
        proj = _in_proj(x2, g_mix[l].reshape(1, D_MODEL), _reorder_in_cols(w_in[l]).astype(BF16))
        bias_sp = jnp.repeat(b_spatial[l].T, A_GROUP_CH, axis=1)
        x2 = _mixers(
            x2, proj, pos2, kct, vc, sinks[l], g_a_v[l].reshape(1, A_WIDTH), w_spatial[l],
            bias_sp, jnp.tile(g_b_q[l], B_HEADS).reshape(1, B_WIDTH),
            jnp.tile(g_b_k[l], B_KV_HEADS).reshape(1, B_KV_WIDTH), invf,
            g_c_q[l].reshape(1, C_HEAD_DIM), bd,
            w_branch_a[l].astype(BF16), w_branch_b[l].astype(BF16),
            w_branch_c[l].astype(BF16), w_out[l].astype(BF16), bn, s_len)
        x2 = _conv_ffn(x2, g_ffn[l].reshape(1, D_MODEL), w_up[l].astype(BF16), conv_w[l],
                       conv_b[l].reshape(1, 2 * D_FF), w_down[l].astype(BF16), s_len)
    return x2.reshape(bn, s_len, D_MODEL)
```

```python
import functools

import jax
import jax.numpy as jnp
import numpy as np
from jax import lax
from jax.experimental import pallas as pl
from jax.experimental.pallas import tpu as pltpu

F32 = jnp.float32
BF16 = jnp.bfloat16

D_MODEL = 2048
MEM_LEN = 256
EPS = 1e-6
CHUNK = 128
A_GROUPS = 4
A_GROUP_CH = 128
A_WIDTH = A_GROUPS * A_GROUP_CH
WINDOW = 128
B_HEADS = 16
B_KV_HEADS = 2
B_REP = B_HEADS // B_KV_HEADS
B_HEAD_DIM = 64
B_WIDTH = B_HEADS * B_HEAD_DIM
B_KV_WIDTH = B_KV_HEADS * B_HEAD_DIM
ROPE_DIM = B_HEAD_DIM // 4
ROPE_HALF = ROPE_DIM // 2
ROPE_THETA = 500000.0
C_HEADS = 4
C_HEAD_DIM = 128
C_WIDTH = C_HEADS * C_HEAD_DIM
N_BRANCH = 3
D_FF = 5632
CONV_W = 3
MAIN_COLS = 2 * A_WIDTH + B_WIDTH + 2 * B_KV_WIDTH + C_WIDTH
IN_COLS = MAIN_COLS + N_BRANCH * D_MODEL

LANES = 128
SUBLANES = 8
MXU_DIM = 256
VMEM_LIMIT_BYTES = 56 * 1024 * 1024

IN_TM = 1024
IN_TN = 1280
MIX_TM = 256
FFN_TM = 512
FFN_FC = 512

COL_QB = N_BRANCH * D_MODEL
COL_UV = COL_QB + B_WIDTH
COL_QC = COL_UV + 2 * A_WIDTH
COL_KV = COL_QC + C_WIDTH


def _rms(x, g):
    ms = jnp.mean(x * x, axis=-1, keepdims=True)
    return x * lax.rsqrt(ms + EPS) * g


def _mem_kv_kernel(mem_ref, g_ref, w_ref, gk_ref, kt_ref, v_ref):
    mh = _rms(mem_ref[0], g_ref[...]).astype(BF16)
    kv = jnp.dot(mh, w_ref[...], preferred_element_type=F32)
    for h in range(C_HEADS):
        k = _rms(kv[:, h * C_HEAD_DIM:(h + 1) * C_HEAD_DIM], gk_ref[...])
        kt_ref[0, h * C_HEAD_DIM:(h + 1) * C_HEAD_DIM, :] = k.T.astype(BF16)
    v_ref[0] = kv[:, C_WIDTH:].astype(BF16)


def _mem_kv(mem, g_mem, w_kv, g_c_k):
    bn = mem.shape[0]
    return pl.pallas_call(
        _mem_kv_kernel,
        grid=(bn,),
        in_specs=[
            pl.BlockSpec((1, MEM_LEN, D_MODEL), lambda b: (b, 0, 0)),
            pl.BlockSpec((1, D_MODEL), lambda b: (0, 0)),
            pl.BlockSpec((D_MODEL, 2 * C_WIDTH), lambda b: (0, 0)),
            pl.BlockSpec((1, C_HEAD_DIM), lambda b: (0, 0)),
        ],
        out_specs=[
            pl.BlockSpec((1, C_WIDTH, MEM_LEN), lambda b: (b, 0, 0)),
            pl.BlockSpec((1, MEM_LEN, C_WIDTH), lambda b: (b, 0, 0)),
        ],
        out_shape=[
            jax.ShapeDtypeStruct((bn, C_WIDTH, MEM_LEN), BF16),
            jax.ShapeDtypeStruct((bn, MEM_LEN, C_WIDTH), BF16),
        ],
        compiler_params=pltpu.CompilerParams(
            dimension_semantics=("arbitrary",), vmem_limit_bytes=VMEM_LIMIT_BYTES),
        name="mem_kv",
    )(mem, g_mem, w_kv, g_c_k)


def _in_proj_kernel(x_ref, g_ref, w_ref, o_ref, h_ref):
    j = pl.program_id(1)

    @pl.when(j == 0)
    def _():
        h_ref[...] = _rms(x_ref[...], g_ref[...]).astype(BF16)

    def tile():
        return jnp.dot(h_ref[...], w_ref[...], preferred_element_type=F32)

    gate_tiles, gate_rem = divmod(COL_QB, IN_TN)
    uv_tile, uv_off = divmod(COL_UV, IN_TN)
    uv_end = COL_QC - (uv_tile + 1) * IN_TN
    assert gate_rem > 0 and uv_tile == gate_tiles + 1 and 0 < uv_end < IN_TN
    assert uv_tile + 2 == IN_COLS // IN_TN

    @pl.when(j < gate_tiles)
    def _():
        o_ref[...] = jax.nn.sigmoid(tile()).astype(o_ref.dtype)

    @pl.when(j == gate_tiles)
    def _():
        acc = tile()
        o_ref[:, :gate_rem] = jax.nn.sigmoid(acc[:, :gate_rem]).astype(o_ref.dtype)
        o_ref[:, gate_rem:] = acc[:, gate_rem:].astype(o_ref.dtype)

    @pl.when(j == uv_tile)
    def _():
        acc = tile()
        o_ref[:, :uv_off] = acc[:, :uv_off].astype(o_ref.dtype)
        o_ref[:, uv_off:] = jax.nn.gelu(acc[:, uv_off:]).astype(o_ref.dtype)

    @pl.when(j == uv_tile + 1)
    def _():
        acc = tile()
        o_ref[:, :uv_end] = jax.nn.gelu(acc[:, :uv_end]).astype(o_ref.dtype)
        o_ref[:, uv_end:] = acc[:, uv_end:].astype(o_ref.dtype)


def _in_proj(x2, g_mix, w_in_r):
    t = x2.shape[0]
    return pl.pallas_call(
        _in_proj_kernel,
        grid=(t // IN_TM, IN_COLS // IN_TN),
        in_specs=[
            pl.BlockSpec((IN_TM, D_MODEL), lambda i, j: (i, 0)),
            pl.BlockSpec((1, D_MODEL), lambda i, j: (0, 0)),
            pl.BlockSpec((D_MODEL, IN_TN), lambda i, j: (0, j)),
        ],
        out_specs=pl.BlockSpec((IN_TM, IN_TN), lambda i, j: (i, j)),
        out_shape=jax.ShapeDtypeStruct((t, IN_COLS), BF16),
        scratch_shapes=[pltpu.VMEM((IN_TM, D_MODEL), BF16)],
        compiler_params=pltpu.CompilerParams(
            dimension_semantics=("arbitrary", "arbitrary"),
            vmem_limit_bytes=VMEM_LIMIT_BYTES),
        name="in_proj",
    )(x2, g_mix, w_in_r)


def _rope_tables(pos, invf):
    ang = pos * invf
    cos = jnp.cos(ang)
    sin = jnp.sin(ang)
    lane = lax.broadcasted_iota(jnp.int32, ang.shape, 1) % B_HEAD_DIM
    s_lo = jnp.where(lane < ROPE_HALF, -sin, 0.0)
    s_hi = jnp.where(lane >= ROPE_HALF, sin, 0.0)
    return cos, s_lo, s_hi


def _rope(z, tabs):
    cos, s_lo, s_hi = tabs
    return (z * cos + pltpu.roll(z, LANES - ROPE_HALF, 1) * s_lo
            + pltpu.roll(z, ROPE_HALF, 1) * s_hi)


def _head_norm(z, bd, g):
    ssq = jnp.dot((z * z).astype(BF16), bd, preferred_element_type=F32)
    return z * lax.rsqrt(ssq * (1.0 / B_HEAD_DIM) + EPS) * g


def _mixer_kernel(sinks_ref, x_ref, g0_ref, g1_ref, g2_ref, qb_ref, uv_ref, qc_ref,
                  kv_ref, pos_ref, kct_ref, vc_ref,
                  gav_ref, wsp_ref, bsp_ref, gq_ref, gk_ref, invf_ref, gcq_ref, bd_ref,
                  wa_ref, wb_ref, wc_ref, wo_ref, o_ref, y_ref, kprev_ref, vprev_ref):
    tm = x_ref.shape[0]
    n_chunks = tm // CHUNK
    tile = pl.program_id(1)
    bd = bd_ref[...]
    bd_kv = bd[:B_KV_WIDTH, :B_KV_WIDTH]

    @pl.when(tile == 0)
    def _():
        kprev_ref[...] = jnp.zeros_like(kprev_ref)
        vprev_ref[...] = jnp.zeros_like(vprev_ref)

    u = uv_ref[:, :A_WIDTH].astype(F32)
    v = _rms(uv_ref[:, A_WIDTH:].astype(F32), gav_ref[...]).astype(BF16)
    row = lax.broadcasted_iota(jnp.int32, (CHUNK, CHUNK), 0)
    col = lax.broadcasted_iota(jnp.int32, (CHUNK, CHUNK), 1)
    w_sp = [jnp.where(row >= col, wsp_ref[g], 0.0).astype(BF16) for g in range(A_GROUPS)]
    for c in range(n_chunks):
        rows = slice(c * CHUNK, (c + 1) * CHUNK)
        s = jnp.concatenate(
            [jnp.dot(w_sp[g], v[rows, g * A_GROUP_CH:(g + 1) * A_GROUP_CH],
                     preferred_element_type=F32) for g in range(A_GROUPS)], axis=1)
        y_ref[rows, 0:A_WIDTH] = (u[rows] * (s + bsp_ref[...])).astype(BF16)

    tabs = _rope_tables(pos_ref[...], invf_ref[...])
    k_cur = _rope(_head_norm(kv_ref[:, :B_KV_WIDTH].astype(F32), bd_kv, gk_ref[...]), tabs)
    kt_all = jnp.concatenate([kprev_ref[...], k_cur], axis=0).T.astype(BF16)
    v_all = jnp.concatenate([vprev_ref[...], kv_ref[:, B_KV_WIDTH:]], axis=0)
    kprev_ref[...] = k_cur[tm - WINDOW:]
    vprev_ref[...] = kv_ref[tm - WINDOW:, B_KV_WIDTH:]

    qb = qb_ref[...].astype(F32)
    scale = B_HEAD_DIM ** -0.5
    q_slabs = []
    for j in range(B_WIDTH // MXU_DIM):
        z = _head_norm(qb[:, j * MXU_DIM:(j + 1) * MXU_DIM], bd,
                       gq_ref[:, j * MXU_DIM:(j + 1) * MXU_DIM])
        for half in range(MXU_DIM // LANES):
            zz = _rope(z[:, half * LANES:(half + 1) * LANES], tabs) * scale
            q_slabs.append(zz.astype(BF16))

    upper = col > row
    for c in range(n_chunks):
        rows = slice(c * CHUNK, (c + 1) * CHUNK)
        for h in range(B_KV_HEADS):
            kt = kt_all[h * B_HEAD_DIM:(h + 1) * B_HEAD_DIM, c * CHUNK:c * CHUNK + 2 * WINDOW]
            v2 = v_all[c * CHUNK:c * CHUNK + 2 * WINDOW, h * B_HEAD_DIM:(h + 1) * B_HEAD_DIM]
            q_stack = jnp.concatenate(
                [q_slabs[(h * B_REP + r) // 2][rows, (r % 2) * B_HEAD_DIM:(r % 2 + 1) * B_HEAD_DIM]
                 for r in range(B_REP)], axis=0)
            s_all = jnp.dot(q_stack, kt, preferred_element_type=F32)
            p_list, l_list = [], []
            for r in range(B_REP):
                s_prev = s_all[r * CHUNK:(r + 1) * CHUNK, :WINDOW]
                if c == 0:
                    s_prev = jnp.where(tile > 0, s_prev, -jnp.inf)
                s = jnp.where(upper, s_prev, s_all[r * CHUNK:(r + 1) * CHUNK, WINDOW:])
                sink = sinks_ref[h * B_REP + r]
                m = jnp.maximum(jnp.max(s, axis=-1, keepdims=True), sink)
                p = jnp.exp(s - m)
                l_list.append(jnp.sum(p, axis=-1, keepdims=True) + jnp.exp(sink - m))
                p_list.append(jnp.concatenate(
                    [jnp.where(upper, p, 0.0), jnp.where(upper, 0.0, p)], axis=1).astype(BF16))
            o_all = jnp.dot(jnp.concatenate(p_list, axis=0), v2,
                            preferred_element_type=F32)
            for r in range(0, B_REP, 2):
                o0 = o_all[r * CHUNK:(r + 1) * CHUNK] / l_list[r]
                o1 = o_all[(r + 1) * CHUNK:(r + 2) * CHUNK] / l_list[r + 1]
                c0 = A_WIDTH + (h * B_REP + r) * B_HEAD_DIM
                y_ref[rows, c0:c0 + 2 * B_HEAD_DIM] = jnp.concatenate(
                    [o0, o1], axis=1).astype(BF16)

    qc = qc_ref[...].astype(F32)
    c_scale = C_HEAD_DIM ** -0.5
    for h in range(C_HEADS):
        cols = slice(h * C_HEAD_DIM, (h + 1) * C_HEAD_DIM)
        qn = _rms(qc[:, cols], gcq_ref[...]).astype(BF16)
        s = jnp.dot(qn, kct_ref[0, cols, :], preferred_element_type=F32) * c_scale
        m = jnp.max(s, axis=-1, keepdims=True)
        p = jnp.exp(s - m)
        l = jnp.sum(p, axis=-1, keepdims=True)
        o = jnp.dot(p.astype(BF16), vc_ref[0, :, cols], preferred_element_type=F32) / l
        c0 = A_WIDTH + B_WIDTH + h * C_HEAD_DIM
        y_ref[:, c0:c0 + C_HEAD_DIM] = o.astype(BF16)

    za = jnp.dot(y_ref[:, 0:A_WIDTH], wa_ref[...], preferred_element_type=F32)
    merged = g0_ref[...].astype(F32) * za
    zb = jnp.dot(y_ref[:, A_WIDTH:A_WIDTH + B_WIDTH], wb_ref[...], preferred_element_type=F32)
    merged = merged + g1_ref[...].astype(F32) * zb
    zc = jnp.dot(y_ref[:, A_WIDTH + B_WIDTH:], wc_ref[...], preferred_element_type=F32)
    merged = merged + g2_ref[...].astype(F32) * zc
    o_ref[...] = x_ref[...] + jnp.dot(merged.astype(BF16), wo_ref[...],
                                      preferred_element_type=F32)


def _mixers(x2, proj, pos2, kct, vc, sinks, g_a_v, w_sp, bias_sp, gq_t, gk_t, invf, gcq,
            bd, wa, wb, wc, wo, bn, s_len):
    t = x2.shape[0]
    tm = MIX_TM
    tps = s_len // tm

    def row(b, i):
        return b * tps + i

    def const(*idx):
        return lambda b, i: idx

    in_specs = [
        pl.BlockSpec(memory_space=pltpu.SMEM),
        pl.BlockSpec((tm, D_MODEL), lambda b, i: (row(b, i), 0)),
        pl.BlockSpec((tm, D_MODEL), lambda b, i: (row(b, i), 0)),
        pl.BlockSpec((tm, D_MODEL), lambda b, i: (row(b, i), 1)),
        pl.BlockSpec((tm, D_MODEL), lambda b, i: (row(b, i), 2)),
        pl.BlockSpec((tm, B_WIDTH), lambda b, i: (row(b, i), COL_QB // B_WIDTH)),
        pl.BlockSpec((tm, 2 * A_WIDTH), lambda b, i: (row(b, i), COL_UV // (2 * A_WIDTH))),
        pl.BlockSpec((tm, C_WIDTH), lambda b, i: (row(b, i), COL_QC // C_WIDTH)),
        pl.BlockSpec((tm, 2 * B_KV_WIDTH), lambda b, i: (row(b, i), COL_KV // (2 * B_KV_WIDTH))),
        pl.BlockSpec((tm, 1), lambda b, i: (row(b, i), 0)),
        pl.BlockSpec((1, C_WIDTH, MEM_LEN), lambda b, i: (b, 0, 0)),
        pl.BlockSpec((1, MEM_LEN, C_WIDTH), lambda b, i: (b, 0, 0)),
        pl.BlockSpec((1, A_WIDTH), const(0, 0)),
        pl.BlockSpec((A_GROUPS, CHUNK, CHUNK), const(0, 0, 0)),
        pl.BlockSpec((CHUNK, A_WIDTH), const(0, 0)),
        pl.BlockSpec((1, B_WIDTH), const(0, 0)),
        pl.BlockSpec((1, B_KV_WIDTH), const(0, 0)),
        pl.BlockSpec((1, LANES), const(0, 0)),
        pl.BlockSpec((1, C_HEAD_DIM), const(0, 0)),
        pl.BlockSpec((MXU_DIM, MXU_DIM), const(0, 0)),
        pl.BlockSpec((A_WIDTH, D_MODEL), const(0, 0), pipeline_mode=pl.Buffered(1)),
        pl.BlockSpec((B_WIDTH, D_MODEL), const(0, 0), pipeline_mode=pl.Buffered(1)),
        pl.BlockSpec((C_WIDTH, D_MODEL), const(0, 0), pipeline_mode=pl.Buffered(1)),
        pl.BlockSpec((D_MODEL, D_MODEL), const(0, 0), pipeline_mode=pl.Buffered(1)),
    ]
    return pl.pallas_call(
        _mixer_kernel,
        grid=(bn, tps),
        in_specs=in_specs,
        out_specs=pl.BlockSpec((tm, D_MODEL), lambda b, i: (row(b, i), 0)),
        out_shape=jax.ShapeDtypeStruct((t, D_MODEL), F32),
        scratch_shapes=[
            pltpu.VMEM((tm, D_MODEL), BF16),
            pltpu.VMEM((WINDOW, B_KV_WIDTH), F32),
            pltpu.VMEM((WINDOW, B_KV_WIDTH), BF16),
        ],
        compiler_params=pltpu.CompilerParams(
            dimension_semantics=("arbitrary", "arbitrary"),
            vmem_limit_bytes=VMEM_LIMIT_BYTES),
        name="mixers",
    )(sinks, x2, proj, proj, proj, proj, proj, proj, proj, pos2, kct, vc,
      g_a_v, w_sp, bias_sp, gq_t, gk_t, invf, gcq, bd, wa, wb, wc, wo)


def _ffn_kernel(tiles_per_seq, x_ref, g_ref, wu_ref, cw_ref, cb_ref, wd_ref, o_ref,
                h_ref, tail_ref, ubuf_ref, acc_ref):
    i = pl.program_id(0)
    f = pl.program_id(1)
    nf = pl.num_programs(1) - 1
    tm = x_ref.shape[0]
    fc = wd_ref.shape[0]
    n_pieces = 2 * fc // MXU_DIM
    rows_pp = tm // n_pieces

    def up_piece(k):
        slot = f % 2
        cols = slice(k * MXU_DIM, (k + 1) * MXU_DIM)
        up = jnp.dot(h_ref[...], wu_ref[:, cols], preferred_element_type=F32)
        seq_start = i % tiles_per_seq == 0
        ubuf_ref[slot, 0:SUBLANES, cols] = jnp.where(seq_start, 0.0, tail_ref[f, :, cols])
        ubuf_ref[slot, SUBLANES:, cols] = up
        tail_ref[f, :, cols] = up[tm - SUBLANES:]

    def gate_piece(k):
        slot = (f + 1) % 2
        r0 = SUBLANES + k * rows_pp
        cw = cw_ref[...]
        conv = (cw[0:1] * ubuf_ref[slot, r0 - 2:r0 - 2 + rows_pp, :]
                + cw[1:2] * ubuf_ref[slot, r0 - 1:r0 - 1 + rows_pp, :]
                + cw[2:3] * ubuf_ref[slot, r0:r0 + rows_pp, :] + cb_ref[...])
        return (jax.nn.silu(conv[:, :fc]) * conv[:, fc:]).astype(BF16)

    @pl.when(f == 0)
    def _():
        x = x_ref[...]
        h_ref[...] = _rms(x, g_ref[...]).astype(BF16)
        acc_ref[...] = x

        @pl.when(i == 0)
        def _():
            tail_ref[...] = jnp.zeros_like(tail_ref)

        for k in range(n_pieces):
            up_piece(k)

    @pl.when((f > 0) & (f < nf))
    def _():
        acts = []
        for k in range(n_pieces):
            up_piece(k)
            acts.append(gate_piece(k))
        act = jnp.concatenate(acts, axis=0)
        acc_ref[...] += jnp.dot(act, wd_ref[...], preferred_element_type=F32)

    @pl.when(f == nf)
    def _():
        act = jnp.concatenate([gate_piece(k) for k in range(n_pieces)], axis=0)
        o_ref[...] = acc_ref[...] + jnp.dot(act, wd_ref[...], preferred_element_type=F32)


def _interleave_ffn_cols(w, fc):
    lead = w.shape[:-1]
    w = w.reshape(lead + (2, D_FF // fc, fc))
    return jnp.swapaxes(w, -3, -2).reshape(lead + (2 * D_FF,))


def _conv_ffn(x2, g_ffn, w_up_r, conv_w_r, conv_b_r, w_down, s_len):
    t = x2.shape[0]
    tm, fc = FFN_TM, FFN_FC
    nf = D_FF // fc

    def up_idx(f):
        return jnp.minimum(f, nf - 1)

    def down_idx(f):
        return jnp.maximum(f - 1, 0)

    return pl.pallas_call(
        functools.partial(_ffn_kernel, s_len // tm),
        grid=(t // tm, nf + 1),
        in_specs=[
            pl.BlockSpec((tm, D_MODEL), lambda i, f: (i, 0)),
            pl.BlockSpec((1, D_MODEL), lambda i, f: (0, 0)),
            pl.BlockSpec((D_MODEL, 2 * fc), lambda i, f: (0, up_idx(f))),
            pl.BlockSpec((CONV_W, 2 * fc), lambda i, f: (0, down_idx(f))),
            pl.BlockSpec((1, 2 * fc), lambda i, f: (0, down_idx(f))),
            pl.BlockSpec((fc, D_MODEL), lambda i, f: (down_idx(f), 0)),
        ],
        out_specs=pl.BlockSpec((tm, D_MODEL), lambda i, f: (i, 0)),
        out_shape=jax.ShapeDtypeStruct((t, D_MODEL), F32),
        scratch_shapes=[
            pltpu.VMEM((tm, D_MODEL), BF16),
            pltpu.VMEM((nf, SUBLANES, 2 * fc), F32),
            pltpu.VMEM((2, tm + SUBLANES, 2 * fc), F32),
            pltpu.VMEM((tm, D_MODEL), F32),
        ],
        compiler_params=pltpu.CompilerParams(
            dimension_semantics=("arbitrary", "arbitrary"),
            vmem_limit_bytes=VMEM_LIMIT_BYTES),
        name="conv_ffn",
    )(x2, g_ffn, w_up_r, conv_w_r, conv_b_r, w_down)


def _reorder_in_cols(w_in):
    o = np.cumsum([0, A_WIDTH, A_WIDTH, B_WIDTH, B_KV_WIDTH, B_KV_WIDTH, C_WIDTH]).tolist()
    u_a, v_a, q_b, k_b, v_b, q_c = (w_in[:, o[n]:o[n + 1]] for n in range(6))
    parts = [w_in[:, MAIN_COLS:], q_b, u_a, v_a, q_c, k_b, v_b]
    return jnp.concatenate([p.astype(BF16) for p in parts], axis=1)


def kernel(x, mem, positions, g_mix, w_in, g_a_v, w_spatial, b_spatial, g_b_q, g_b_k, sinks,
           g_mem, w_mem_kv, g_c_q, g_c_k, w_branch_a, w_branch_b, w_branch_c, w_out, g_ffn,
           w_up, conv_w, conv_b, w_down):
    bn, s_len, _ = x.shape
    depth = w_in.shape[0]
    assert s_len % MIX_TM == 0 and s_len % FFN_TM == 0 and (bn * s_len) % IN_TM == 0
    t = bn * s_len
    x2 = x.reshape(t, D_MODEL)
    pos2 = positions.astype(F32).reshape(t, 1)

    inv = ROPE_THETA ** (-jnp.arange(ROPE_HALF, dtype=F32) / ROPE_HALF)
    inv_head = jnp.concatenate([inv, inv, jnp.zeros((B_HEAD_DIM - ROPE_DIM,), F32)])
    invf = jnp.tile(inv_head, LANES // B_HEAD_DIM).reshape(1, LANES)
    grp = np.arange(MXU_DIM) // B_HEAD_DIM
    bd = jnp.asarray(grp[:, None] == grp[None, :], dtype=BF16)

    for l in range(depth):
        kct, vc = _mem_kv(mem, g_mem[l].reshape(1, D_MODEL), w_mem_kv[l].astype(BF16),
                          g_c_k[l].reshape(1, C_HEAD_DIM))
        proj = _in_proj(x2, g_mix[l].reshape(1, D_MODEL), _reorder_in_cols(w_in[l]))
        bias_sp = jnp.repeat(b_spatial[l].T, A_GROUP_CH, axis=1)
        x2 = _mixers(
            x2, proj, pos2, kct, vc, sinks[l], g_a_v[l].reshape(1, A_WIDTH), w_spatial[l],
            bias_sp, jnp.tile(g_b_q[l], B_HEADS).reshape(1, B_WIDTH),
            jnp.tile(g_b_k[l], B_KV_HEADS).reshape(1, B_KV_WIDTH), invf,
            g_c_q[l].reshape(1, C_HEAD_DIM), bd,
            w_branch_a[l].astype(BF16), w_branch_b[l].astype(BF16),
            w_branch_c[l].astype(BF16), w_out[l].astype(BF16), bn, s_len)
        x2 = _conv_ffn(x2, g_ffn[l].reshape(1, D_MODEL),
                       _interleave_ffn_cols(w_up[l].astype(BF16), FFN_FC),
                       _interleave_ffn_cols(conv_w[l], FFN_FC),
                       _interleave_ffn_cols(conv_b[l].reshape(1, 2 * D_FF), FFN_FC),
                       w_down[l].astype(BF16), s_len)
    return x2.reshape(bn, s_len, D_MODEL)
```

```python
import functools

import jax
import jax.numpy as jnp
import numpy as np
from jax import lax
from jax.experimental import pallas as pl
from jax.experimental.pallas import tpu as pltpu

F32 = jnp.float32
BF16 = jnp.bfloat16

D_MODEL = 2048
MEM_LEN = 256
EPS = 1e-6
CHUNK = 128
A_GROUPS = 4
A_GROUP_CH = 128
A_WIDTH = A_GROUPS * A_GROUP_CH
WINDOW = 128
B_HEADS = 16
B_KV_HEADS = 2
B_REP = B_HEADS // B_KV_HEADS
B_HEAD_DIM = 64
B_WIDTH = B_HEADS * B_HEAD_DIM
B_KV_WIDTH = B_KV_HEADS * B_HEAD_DIM
ROPE_DIM = B_HEAD_DIM // 4
ROPE_HALF = ROPE_DIM // 2
ROPE_THETA = 500000.0
C_HEADS = 4
C_HEAD_DIM = 128
C_WIDTH = C_HEADS * C_HEAD_DIM
N_BRANCH = 3
D_FF = 5632
CONV_W = 3
MAIN_COLS = 2 * A_WIDTH + B_WIDTH + 2 * B_KV_WIDTH + C_WIDTH
IN_COLS = MAIN_COLS + N_BRANCH * D_MODEL

LANES = 128
SUBLANES = 8
MXU_DIM = 256
VMEM_LIMIT_BYTES = 56 * 1024 * 1024

IN_TM = 1024
IN_TN = 1280
MIX_TM = 256
FFN_TM = 512
FFN_FC = 512

COL_QB = N_BRANCH * D_MODEL
COL_UV = COL_QB + B_WIDTH
COL_QC = COL_UV + 2 * A_WIDTH
COL_KV = COL_QC + C_WIDTH


def _sigmoid(x):
    return 0.5 * jnp.tanh(0.5 * x) + 0.5


def _rms(x, g):
    ms = jnp.mean(x * x, axis=-1, keepdims=True)
    return x * lax.rsqrt(ms + EPS) * g


def _mem_kv_kernel(mem_ref, g_ref, w_ref, gk_ref, kt_ref, v_ref):
    mh = _rms(mem_ref[0], g_ref[...]).astype(BF16)
    kv = jnp.dot(mh, w_ref[...], preferred_element_type=F32)
    for h in range(C_HEADS):
        k = _rms(kv[:, h * C_HEAD_DIM:(h + 1) * C_HEAD_DIM], gk_ref[...])
        kt_ref[0, h * C_HEAD_DIM:(h + 1) * C_HEAD_DIM, :] = k.T.astype(BF16)
    v_ref[0] = kv[:, C_WIDTH:].astype(BF16)


def _mem_kv(mem, g_mem, w_kv, g_c_k):
    bn = mem.shape[0]
    return pl.pallas_call(
        _mem_kv_kernel,
        grid=(bn,),
        in_specs=[
            pl.BlockSpec((1, MEM_LEN, D_MODEL), lambda b: (b, 0, 0)),
            pl.BlockSpec((1, D_MODEL), lambda b: (0, 0)),
            pl.BlockSpec((D_MODEL, 2 * C_WIDTH), lambda b: (0, 0)),
            pl.BlockSpec((1, C_HEAD_DIM), lambda b: (0, 0)),
        ],
        out_specs=[
            pl.BlockSpec((1, C_WIDTH, MEM_LEN), lambda b: (b, 0, 0)),
            pl.BlockSpec((1, MEM_LEN, C_WIDTH), lambda b: (b, 0, 0)),
        ],
        out_shape=[
            jax.ShapeDtypeStruct((bn, C_WIDTH, MEM_LEN), BF16),
            jax.ShapeDtypeStruct((bn, MEM_LEN, C_WIDTH), BF16),
        ],
        compiler_params=pltpu.CompilerParams(
            dimension_semantics=("arbitrary",), vmem_limit_bytes=VMEM_LIMIT_BYTES),
        name="mem_kv",
    )(mem, g_mem, w_kv, g_c_k)


def _in_proj_kernel(x_ref, g_ref, w_ref, o_ref, h_ref):
    j = pl.program_id(1)

    @pl.when(j == 0)
    def _():
        h_ref[...] = _rms(x_ref[...], g_ref[...]).astype(BF16)

    def tile():
        return jnp.dot(h_ref[...], w_ref[...], preferred_element_type=F32)

    gate_tiles, gate_rem = divmod(COL_QB, IN_TN)
    uv_tile, uv_off = divmod(COL_UV, IN_TN)
    uv_end = COL_QC - (uv_tile + 1) * IN_TN
    assert gate_rem > 0 and uv_tile == gate_tiles + 1 and 0 < uv_end < IN_TN
    assert uv_tile + 2 == IN_COLS // IN_TN

    @pl.when(j < gate_tiles)
    def _():
        o_ref[...] = _sigmoid(tile()).astype(o_ref.dtype)

    @pl.when(j == gate_tiles)
    def _():
        acc = tile()
        o_ref[:, :gate_rem] = _sigmoid(acc[:, :gate_rem]).astype(o_ref.dtype)
        o_ref[:, gate_rem:] = acc[:, gate_rem:].astype(o_ref.dtype)

    @pl.when(j == uv_tile)
    def _():
        acc = tile()
        o_ref[:, :uv_off] = acc[:, :uv_off].astype(o_ref.dtype)
        o_ref[:, uv_off:] = jax.nn.gelu(acc[:, uv_off:]).astype(o_ref.dtype)

    @pl.when(j == uv_tile + 1)
    def _():
        acc = tile()
        o_ref[:, :uv_end] = jax.nn.gelu(acc[:, :uv_end]).astype(o_ref.dtype)
        o_ref[:, uv_end:] = acc[:, uv_end:].astype(o_ref.dtype)


def _in_proj(x2, g_mix, w_in_r):
    t = x2.shape[0]
    return pl.pallas_call(
        _in_proj_kernel,
        grid=(t // IN_TM, IN_COLS // IN_TN),
        in_specs=[
            pl.BlockSpec((IN_TM, D_MODEL), lambda i, j: (i, 0)),
            pl.BlockSpec((1, D_MODEL), lambda i, j: (0, 0)),
            pl.BlockSpec((D_MODEL, IN_TN), lambda i, j: (0, j)),
        ],
        out_specs=pl.BlockSpec((IN_TM, IN_TN), lambda i, j: (i, j)),
        out_shape=jax.ShapeDtypeStruct((t, IN_COLS), BF16),
        scratch_shapes=[pltpu.VMEM((IN_TM, D_MODEL), BF16)],
        compiler_params=pltpu.CompilerParams(
            dimension_semantics=("arbitrary", "arbitrary"),
            vmem_limit_bytes=VMEM_LIMIT_BYTES),
        name="in_proj",
    )(x2, g_mix, w_in_r)


def _rope_tables(pos, invf):
    ang = pos * invf
    cos = jnp.cos(ang)
    sin = jnp.sin(ang)
    lane = lax.broadcasted_iota(jnp.int32, ang.shape, 1) % B_HEAD_DIM
    s_lo = jnp.where(lane < ROPE_HALF, -sin, 0.0)
    s_hi = jnp.where(lane >= ROPE_HALF, sin, 0.0)
    return cos, s_lo, s_hi


def _rope(z, tabs):
    cos, s_lo, s_hi = tabs
    return (z * cos + pltpu.roll(z, LANES - ROPE_HALF, 1) * s_lo
            + pltpu.roll(z, ROPE_HALF, 1) * s_hi)


def _head_norm(z, bd, g):
    ssq = jnp.dot((z * z).astype(BF16), bd, preferred_element_type=F32)
    return z * lax.rsqrt(ssq * (1.0 / B_HEAD_DIM) + EPS) * g


def _mixer_kernel(sinks_ref, x_ref, g0_ref, g1_ref, g2_ref, qb_ref, uv_ref, qc_ref,
                  kv_ref, pos_ref, kct_ref, vc_ref,
                  gav_ref, wsp_ref, bsp_ref, gq_ref, gk_ref, invf_ref, gcq_ref, bd_ref,
                  wa_ref, wb_ref, wc_ref, wo_ref, o_ref, y_ref, kprev_ref, vprev_ref):
    tm = x_ref.shape[0]
    n_chunks = tm // CHUNK
    tile = pl.program_id(1)
    bd = bd_ref[...]
    bd_kv = bd[:B_KV_WIDTH, :B_KV_WIDTH]

    @pl.when(tile == 0)
    def _():
        kprev_ref[...] = jnp.zeros_like(kprev_ref)
        vprev_ref[...] = jnp.zeros_like(vprev_ref)

    u = uv_ref[:, :A_WIDTH].astype(F32)
    v = _rms(uv_ref[:, A_WIDTH:].astype(F32), gav_ref[...]).astype(BF16)
    row = lax.broadcasted_iota(jnp.int32, (CHUNK, CHUNK), 0)
    col = lax.broadcasted_iota(jnp.int32, (CHUNK, CHUNK), 1)
    w_sp = [jnp.where(row >= col, wsp_ref[g], 0.0).astype(BF16) for g in range(A_GROUPS)]
    for c in range(n_chunks):
        rows = slice(c * CHUNK, (c + 1) * CHUNK)
        s = jnp.concatenate(
            [jnp.dot(w_sp[g], v[rows, g * A_GROUP_CH:(g + 1) * A_GROUP_CH],
                     preferred_element_type=F32) for g in range(A_GROUPS)], axis=1)
        y_ref[rows, 0:A_WIDTH] = (u[rows] * (s + bsp_ref[...])).astype(BF16)

    tabs = _rope_tables(pos_ref[...], invf_ref[...])
    k_cur = _rope(_head_norm(kv_ref[:, :B_KV_WIDTH].astype(F32), bd_kv, gk_ref[...]), tabs)
    kt_all = jnp.concatenate([kprev_ref[...], k_cur], axis=0).T.astype(BF16)
    v_all = jnp.concatenate([vprev_ref[...], kv_ref[:, B_KV_WIDTH:]], axis=0)
    kprev_ref[...] = k_cur[tm - WINDOW:]
    vprev_ref[...] = kv_ref[tm - WINDOW:, B_KV_WIDTH:]

    qb = qb_ref[...].astype(F32)
    scale = B_HEAD_DIM ** -0.5
    q_slabs = []
    for j in range(B_WIDTH // MXU_DIM):
        z = _head_norm(qb[:, j * MXU_DIM:(j + 1) * MXU_DIM], bd,
                       gq_ref[:, j * MXU_DIM:(j + 1) * MXU_DIM])
        for half in range(MXU_DIM // LANES):
            zz = _rope(z[:, half * LANES:(half + 1) * LANES], tabs) * scale
            q_slabs.append(zz.astype(BF16))

    upper = col > row
    lane_lo = col < B_HEAD_DIM
    spk = B_REP // 2
    zk = jnp.zeros((B_HEAD_DIM, 2 * WINDOW), BF16)
    zv = jnp.zeros((2 * WINDOW, B_HEAD_DIM), BF16)
    ov = jnp.ones((2 * WINDOW, B_HEAD_DIM), BF16)
    for c in range(n_chunks):
        rows = slice(c * CHUNK, (c + 1) * CHUNK)
        for h in range(B_KV_HEADS):
            kt = kt_all[h * B_HEAD_DIM:(h + 1) * B_HEAD_DIM, c * CHUNK:c * CHUNK + 2 * WINDOW]
            v2 = v_all[c * CHUNK:c * CHUNK + 2 * WINDOW, h * B_HEAD_DIM:(h + 1) * B_HEAD_DIM]
            k_bd = jnp.concatenate([jnp.concatenate([kt, zk], axis=1),
                                    jnp.concatenate([zk, kt], axis=1)], axis=0)
            v_bd = jnp.concatenate([jnp.concatenate([v2, zv, ov, zv], axis=1),
                                    jnp.concatenate([zv, v2, zv, ov], axis=1)], axis=0)
            q_stack = jnp.concatenate([q_slabs[h * spk + j][rows] for j in range(spk)], axis=0)
            s_all = jnp.dot(q_stack, k_bd, preferred_element_type=F32)
            p_rows, e_rows = [], []
            for j in range(spk):
                p_pair, e_pair = [], []
                for half in range(2):
                    blk = s_all[j * CHUNK:(j + 1) * CHUNK,
                                half * 2 * WINDOW:(half + 1) * 2 * WINDOW]
                    s_prev = blk[:, :WINDOW]
                    if c == 0:
                        s_prev = jnp.where(tile > 0, s_prev, -jnp.inf)
                    s = jnp.where(upper, s_prev, blk[:, WINDOW:])
                    sink = sinks_ref[h * B_REP + 2 * j + half]
                    m = jnp.maximum(jnp.max(s, axis=-1, keepdims=True), sink)
                    p = jnp.exp(s - m)
                    e_pair.append(jnp.exp(sink - m))
                    p_pair += [jnp.where(upper, p, 0.0), jnp.where(upper, 0.0, p)]
                p_rows.append(jnp.concatenate(p_pair, axis=1).astype(BF16))
                e_rows.append(jnp.where(lane_lo, e_pair[0], e_pair[1]))
            o_all = jnp.dot(jnp.concatenate(p_rows, axis=0), v_bd,
                            preferred_element_type=F32)
            for j in range(spk):
                o = o_all[j * CHUNK:(j + 1) * CHUNK]
                c0 = A_WIDTH + (h * spk + j) * LANES
                y_ref[rows, c0:c0 + LANES] = (o[:, :LANES] / (o[:, LANES:] + e_rows[j])).astype(BF16)

    qc = qc_ref[...].astype(F32)
    c_scale = C_HEAD_DIM ** -0.5
    for h in range(C_HEADS):
        cols = slice(h * C_HEAD_DIM, (h + 1) * C_HEAD_DIM)
        qn = _rms(qc[:, cols], gcq_ref[...]).astype(BF16)
        s = jnp.dot(qn, kct_ref[0, cols, :], preferred_element_type=F32) * c_scale
        m = jnp.max(s, axis=-1, keepdims=True)
        p = jnp.exp(s - m)
        l = jnp.sum(p, axis=-1, keepdims=True)
        o = jnp.dot(p.astype(BF16), vc_ref[0, :, cols], preferred_element_type=F32) / l
        c0 = A_WIDTH + B_WIDTH + h * C_HEAD_DIM
        y_ref[:, c0:c0 + C_HEAD_DIM] = o.astype(BF16)

    za = jnp.dot(y_ref[:, 0:A_WIDTH], wa_ref[...], preferred_element_type=F32)
    merged = g0_ref[...].astype(F32) * za
    zb = jnp.dot(y_ref[:, A_WIDTH:A_WIDTH + B_WIDTH], wb_ref[...], preferred_element_type=F32)
    merged = merged + g1_ref[...].astype(F32) * zb
    zc = jnp.dot(y_ref[:, A_WIDTH + B_WIDTH:], wc_ref[...], preferred_element_type=F32)
    merged = merged + g2_ref[...].astype(F32) * zc
    o_ref[...] = x_ref[...] + jnp.dot(merged.astype(BF16), wo_ref[...],
                                      preferred_element_type=F32)


def _mixers(x2, proj, pos2, kct, vc, sinks, g_a_v, w_sp, bias_sp, gq_t, gk_t, invf, gcq,
            bd, wa, wb, wc, wo, bn, s_len):
    t = x2.shape[0]
    tm = MIX_TM
    tps = s_len // tm

    def row(b, i):
        return b * tps + i

    def const(*idx):
        return lambda b, i: idx

    in_specs = [
        pl.BlockSpec(memory_space=pltpu.SMEM),
        pl.BlockSpec((tm, D_MODEL), lambda b, i: (row(b, i), 0)),
        pl.BlockSpec((tm, D_MODEL), lambda b, i: (row(b, i), 0)),
        pl.BlockSpec((tm, D_MODEL), lambda b, i: (row(b, i), 1)),
        pl.BlockSpec((tm, D_MODEL), lambda b, i: (row(b, i), 2)),
        pl.BlockSpec((tm, B_WIDTH), lambda b, i: (row(b, i), COL_QB // B_WIDTH)),
        pl.BlockSpec((tm, 2 * A_WIDTH), lambda b, i: (row(b, i), COL_UV // (2 * A_WIDTH))),
        pl.BlockSpec((tm, C_WIDTH), lambda b, i: (row(b, i), COL_QC // C_WIDTH)),
        pl.BlockSpec((tm, 2 * B_KV_WIDTH), lambda b, i: (row(b, i), COL_KV // (2 * B_KV_WIDTH))),
        pl.BlockSpec((tm, 1), lambda b, i: (row(b, i), 0)),
        pl.BlockSpec((1, C_WIDTH, MEM_LEN), lambda b, i: (b, 0, 0)),
        pl.BlockSpec((1, MEM_LEN, C_WIDTH), lambda b, i: (b, 0, 0)),
        pl.BlockSpec((1, A_WIDTH), const(0, 0)),
        pl.BlockSpec((A_GROUPS, CHUNK, CHUNK), const(0, 0, 0)),
        pl.BlockSpec((CHUNK, A_WIDTH), const(0, 0)),
        pl.BlockSpec((1, B_WIDTH), const(0, 0)),
        pl.BlockSpec((1, B_KV_WIDTH), const(0, 0)),
        pl.BlockSpec((1, LANES), const(0, 0)),
        pl.BlockSpec((1, C_HEAD_DIM), const(0, 0)),
        pl.BlockSpec((MXU_DIM, MXU_DIM), const(0, 0)),
        pl.BlockSpec((A_WIDTH, D_MODEL), const(0, 0), pipeline_mode=pl.Buffered(1)),
        pl.BlockSpec((B_WIDTH, D_MODEL), const(0, 0), pipeline_mode=pl.Buffered(1)),
        pl.BlockSpec((C_WIDTH, D_MODEL), const(0, 0), pipeline_mode=pl.Buffered(1)),
        pl.BlockSpec((D_MODEL, D_MODEL), const(0, 0), pipeline_mode=pl.Buffered(1)),
    ]
    return pl.pallas_call(
        _mixer_kernel,
        grid=(bn, tps),
        in_specs=in_specs,
        out_specs=pl.BlockSpec((tm, D_MODEL), lambda b, i: (row(b, i), 0)),
        out_shape=jax.ShapeDtypeStruct((t, D_MODEL), F32),
        scratch_shapes=[
            pltpu.VMEM((tm, D_MODEL), BF16),
            pltpu.VMEM((WINDOW, B_KV_WIDTH), F32),
            pltpu.VMEM((WINDOW, B_KV_WIDTH), BF16),
        ],
        compiler_params=pltpu.CompilerParams(
            dimension_semantics=("arbitrary", "arbitrary"),
            vmem_limit_bytes=VMEM_LIMIT_BYTES),
        name="mixers",
    )(sinks, x2, proj, proj, proj, proj, proj, proj, proj, pos2, kct, vc,
      g_a_v, w_sp, bias_sp, gq_t, gk_t, invf, gcq, bd, wa, wb, wc, wo)


def _ffn_kernel(tiles_per_seq, x_ref, g_ref, wua_ref, wub_ref, cwa_ref, cwb_ref, cba_ref,
                cbb_ref, wd_ref, o_ref, h_ref, tail_ref, ubuf_ref, acc_ref):
    i = pl.program_id(0)
    f = pl.program_id(1)
    nf = pl.num_programs(1) - 1
    tm = x_ref.shape[0]
    fc = wd_ref.shape[0]
    ppw = fc // MXU_DIM
    n_pieces = 2 * ppw
    rows_pp = tm // n_pieces

    def up_piece(k):
        slot = f % 2
        w_ref = wua_ref if k < ppw else wub_ref
        wcols = slice((k % ppw) * MXU_DIM, (k % ppw + 1) * MXU_DIM)
        cols = slice(k * MXU_DIM, (k + 1) * MXU_DIM)
        up = jnp.dot(h_ref[...], w_ref[:, wcols], preferred_element_type=F32)
        seq_start = i % tiles_per_seq == 0
        ubuf_ref[slot, 0:SUBLANES, cols] = jnp.where(seq_start, 0.0, tail_ref[f, :, cols])
        ubuf_ref[slot, SUBLANES:, cols] = up
        tail_ref[f, :, cols] = up[tm - SUBLANES:]

    def gate_piece(k):
        slot = (f + 1) % 2
        r0 = SUBLANES + k * rows_pp
        cw = jnp.concatenate([cwa_ref[...], cwb_ref[...]], axis=1)
        cb = jnp.concatenate([cba_ref[...], cbb_ref[...]], axis=1)
        conv = (cw[0:1] * ubuf_ref[slot, r0 - 2:r0 - 2 + rows_pp, :]
                + cw[1:2] * ubuf_ref[slot, r0 - 1:r0 - 1 + rows_pp, :]
                + cw[2:3] * ubuf_ref[slot, r0:r0 + rows_pp, :] + cb)
        ha = 0.5 * conv[:, :fc]
        return ((ha + ha * jnp.tanh(ha)) * conv[:, fc:]).astype(BF16)

    @pl.when(f == 0)
    def _():
        x = x_ref[...]
        h_ref[...] = _rms(x, g_ref[...]).astype(BF16)
        acc_ref[...] = x

        @pl.when(i == 0)
        def _():
            tail_ref[...] = jnp.zeros_like(tail_ref)

        for k in range(n_pieces):
            up_piece(k)

    @pl.when((f > 0) & (f < nf))
    def _():
        acts = []
        for k in range(n_pieces):
            up_piece(k)
            acts.append(gate_piece(k))
        act = jnp.concatenate(acts, axis=0)
        acc_ref[...] += jnp.dot(act, wd_ref[...], preferred_element_type=F32)

    @pl.when(f == nf)
    def _():
        act = jnp.concatenate([gate_piece(k) for k in range(n_pieces)], axis=0)
        o_ref[...] = acc_ref[...] + jnp.dot(act, wd_ref[...], preferred_element_type=F32)


def _conv_ffn(x2, g_ffn, w_up, conv_w, conv_b, w_down, s_len):
    t = x2.shape[0]
    tm, fc = FFN_TM, FFN_FC
    nf = D_FF // fc

    def up_idx(f):
        return jnp.minimum(f, nf - 1)

    def down_idx(f):
        return jnp.maximum(f - 1, 0)

    return pl.pallas_call(
        functools.partial(_ffn_kernel, s_len // tm),
        grid=(t // tm, nf + 1),
        in_specs=[
            pl.BlockSpec((tm, D_MODEL), lambda i, f: (i, 0)),
            pl.BlockSpec((1, D_MODEL), lambda i, f: (0, 0)),
            pl.BlockSpec((D_MODEL, fc), lambda i, f: (0, up_idx(f))),
            pl.BlockSpec((D_MODEL, fc), lambda i, f: (0, nf + up_idx(f))),
            pl.BlockSpec((CONV_W, fc), lambda i, f: (0, down_idx(f))),
            pl.BlockSpec((CONV_W, fc), lambda i, f: (0, nf + down_idx(f))),
            pl.BlockSpec((1, fc), lambda i, f: (0, down_idx(f))),
            pl.BlockSpec((1, fc), lambda i, f: (0, nf + down_idx(f))),
            pl.BlockSpec((fc, D_MODEL), lambda i, f: (down_idx(f), 0)),
        ],
        out_specs=pl.BlockSpec((tm, D_MODEL), lambda i, f: (i, 0)),
        out_shape=jax.ShapeDtypeStruct((t, D_MODEL), F32),
        scratch_shapes=[
            pltpu.VMEM((tm, D_MODEL), BF16),
            pltpu.VMEM((nf, SUBLANES, 2 * fc), F32),
            pltpu.VMEM((2, tm + SUBLANES, 2 * fc), F32),
            pltpu.VMEM((tm, D_MODEL), F32),
        ],
        compiler_params=pltpu.CompilerParams(
            dimension_semantics=("arbitrary", "arbitrary"),
            vmem_limit_bytes=VMEM_LIMIT_BYTES),
        name="conv_ffn",
    )(x2, g_ffn, w_up, w_up, conv_w, conv_w, conv_b, conv_b, w_down)


def _reorder_in_cols(w_in):
    o = np.cumsum([0, A_WIDTH, A_WIDTH, B_WIDTH, B_KV_WIDTH, B_KV_WIDTH, C_WIDTH]).tolist()
    u_a, v_a, q_b, k_b, v_b, q_c = (w_in[:, o[n]:o[n + 1]] for n in range(6))
    return jnp.concatenate([w_in[:, MAIN_COLS:], q_b, u_a, v_a, q_c, k_b, v_b], axis=1)


def kernel(x, mem, positions, g_mix, w_in, g_a_v, w_spatial, b_spatial, g_b_q, g_b_k, sinks,
           g_mem, w_mem_kv, g_c_q, g_c_k, w_branch_a, w_branch_b, w_branch_c, w_out, g_ffn,
           w_up, conv_w, conv_b, w_down):
    bn, s_len, _ = x.shape
    depth = w_in.shape[0]
    assert s_len % MIX_TM == 0 and s_len % FFN_TM == 0 and (bn * s_len) % IN_TM == 0
    t = bn * s_len
    x2 = x.reshape(t, D_MODEL)
    pos2 = positions.astype(F32).reshape(t, 1)

    inv = ROPE_THETA ** (-jnp.arange(ROPE_HALF, dtype=F32) / ROPE_HALF)
    inv_head = jnp.concatenate([inv, inv, jnp.zeros((B_HEAD_DIM - ROPE_DIM,), F32)])
    invf = jnp.tile(inv_head, LANES // B_HEAD_DIM).reshape(1, LANES)
    grp = np.arange(MXU_DIM) // B_HEAD_DIM
    bd = jnp.asarray(grp[:, None] == grp[None, :], dtype=BF16)

    for l in range(depth):
        kct, vc = _mem_kv(mem, g_mem[l].reshape(1, D_MODEL), w_mem_kv[l].astype(BF16),
                          g_c_k[l].reshape(1, C_HEAD_DIM))
        proj = _in_proj(x2, g_mix[l].reshape(1, D_MODEL), _reorder_in_cols(w_in[l]).astype(BF16))
        bias_sp = jnp.repeat(b_spatial[l].T, A_GROUP_CH, axis=1)
        x2 = _mixers(
            x2, proj, pos2, kct, vc, sinks[l], g_a_v[l].reshape(1, A_WIDTH), w_spatial[l],
            bias_sp, jnp.tile(g_b_q[l], B_HEADS).reshape(1, B_WIDTH),
            jnp.tile(g_b_k[l], B_KV_HEADS).reshape(1, B_KV_WIDTH), invf,
            g_c_q[l].reshape(1, C_HEAD_DIM), bd,
            w_branch_a[l].astype(BF16), w_branch_b[l].astype(BF16),
            w_branch_c[l].astype(BF16), w_out[l].astype(BF16), bn, s_len)
        x2 = _conv_ffn(x2, g_ffn[l].reshape(1, D_MODEL), w_up[l].astype(BF16), conv_w[l],
                       conv_b[l].reshape(1, 2 * D_FF), w_down[l].astype(BF16), s_len)
    return x2.reshape(bn, s_len, D_MODEL)
```

```python
import functools

import jax
import jax.numpy as jnp
import numpy as np
from jax import lax
from jax.experimental import pallas as pl
from jax.experimental.pallas import tpu as pltpu

F32 = jnp.float32
BF16 = jnp.bfloat16

D_MODEL = 2048
MEM_LEN = 256
EPS = 1e-6
CHUNK = 128
A_GROUPS = 4
A_GROUP_CH = 128
A_WIDTH = A_GROUPS * A_GROUP_CH
WINDOW = 128
B_HEADS = 16
B_KV_HEADS = 2
B_REP = B_HEADS // B_KV_HEADS
B_HEAD_DIM = 64
B_WIDTH = B_HEADS * B_HEAD_DIM
B_KV_WIDTH = B_KV_HEADS * B_HEAD_DIM
ROPE_DIM = B_HEAD_DIM // 4
ROPE_HALF = ROPE_DIM // 2
ROPE_THETA = 500000.0
C_HEADS = 4
C_HEAD_DIM = 128
C_WIDTH = C_HEADS * C_HEAD_DIM
N_BRANCH = 3
D_FF = 5632
CONV_W = 3
MAIN_COLS = 2 * A_WIDTH + B_WIDTH + 2 * B_KV_WIDTH + C_WIDTH
IN_COLS = MAIN_COLS + N_BRANCH * D_MODEL

LANES = 128
SUBLANES = 8
MXU_DIM = 256
VMEM_LIMIT_BYTES = 56 * 1024 * 1024

IN_TM = 1024
IN_TN = 1280
MIX_TM = 256
FFN_TM = 1024
FFN_FC = 512

COL_QB = N_BRANCH * D_MODEL
COL_UV = COL_QB + B_WIDTH
COL_QC = COL_UV + 2 * A_WIDTH
COL_KV = COL_QC + C_WIDTH


def _sigmoid(x):
    return 0.5 * jnp.tanh(0.5 * x) + 0.5


def _rms(x, g):
    ms = jnp.mean(x * x, axis=-1, keepdims=True)
    return x * lax.rsqrt(ms + EPS) * g


def _mem_kv_kernel(mem_ref, g_ref, w_ref, gk_ref, kt_ref, v_ref):
    mh = _rms(mem_ref[0], g_ref[...]).astype(BF16)
    kv = jnp.dot(mh, w_ref[...], preferred_element_type=F32)
    for h in range(C_HEADS):
        k = _rms(kv[:, h * C_HEAD_DIM:(h + 1) * C_HEAD_DIM], gk_ref[...])
        kt_ref[0, h * C_HEAD_DIM:(h + 1) * C_HEAD_DIM, :] = k.T.astype(BF16)
    v_ref[0] = kv[:, C_WIDTH:].astype(BF16)


def _mem_kv(mem, g_mem, w_kv, g_c_k):
    bn = mem.shape[0]
    return pl.pallas_call(
        _mem_kv_kernel,
        grid=(bn,),
        in_specs=[
            pl.BlockSpec((1, MEM_LEN, D_MODEL), lambda b: (b, 0, 0)),
            pl.BlockSpec((1, D_MODEL), lambda b: (0, 0)),
            pl.BlockSpec((D_MODEL, 2 * C_WIDTH), lambda b: (0, 0)),
            pl.BlockSpec((1, C_HEAD_DIM), lambda b: (0, 0)),
        ],
        out_specs=[
            pl.BlockSpec((1, C_WIDTH, MEM_LEN), lambda b: (b, 0, 0)),
            pl.BlockSpec((1, MEM_LEN, C_WIDTH), lambda b: (b, 0, 0)),
        ],
        out_shape=[
            jax.ShapeDtypeStruct((bn, C_WIDTH, MEM_LEN), BF16),
            jax.ShapeDtypeStruct((bn, MEM_LEN, C_WIDTH), BF16),
        ],
        compiler_params=pltpu.CompilerParams(
            dimension_semantics=("arbitrary",), vmem_limit_bytes=VMEM_LIMIT_BYTES),
        name="mem_kv",
    )(mem, g_mem, w_kv, g_c_k)


def _cast_plan(n_steps):
    plan = [
        ("w_up", D_MODEL, 2 * D_FF, 32, 0),
        ("w_down", D_FF, D_MODEL, 128, 64),
        ("w_out", D_MODEL, D_MODEL, 64, 0),
        ("w_branch_b", B_WIDTH, D_MODEL, 64, 32),
        ("w_branch_a", A_WIDTH, D_MODEL, 64, 48),
        ("w_branch_c", C_WIDTH, D_MODEL, 64, 56),
    ]
    for _, rows, _, br, start in plan:
        assert rows % br == 0 and start + rows // br <= n_steps
    return plan


def _in_proj_kernel(plan, x_ref, g_ref, w_ref, *refs):
    n_side = len(plan)
    side_in, o_ref, side_out, h_ref = (refs[:n_side], refs[n_side],
                                       refs[n_side + 1:2 * n_side + 1], refs[-1])
    j = pl.program_id(1)
    step = pl.program_id(0) * pl.num_programs(1) + j

    for (_, rows, _, br, start), src, dst in zip(plan, side_in, side_out):
        @pl.when((step >= start) & (step < start + rows // br))
        def _():
            dst[...] = src[...].astype(BF16)

    @pl.when(j == 0)
    def _():
        h_ref[...] = _rms(x_ref[...], g_ref[...]).astype(BF16)

    def tile():
        return jnp.dot(h_ref[...], w_ref[...], preferred_element_type=F32)

    gate_tiles, gate_rem = divmod(COL_QB, IN_TN)
    uv_tile, uv_off = divmod(COL_UV, IN_TN)
    uv_end = COL_QC - (uv_tile + 1) * IN_TN
    assert gate_rem > 0 and uv_tile == gate_tiles + 1 and 0 < uv_end < IN_TN
    assert uv_tile + 2 == IN_COLS // IN_TN

    @pl.when(j < gate_tiles)
    def _():
        o_ref[...] = _sigmoid(tile()).astype(o_ref.dtype)

    @pl.when(j == gate_tiles)
    def _():
        acc = tile()
        o_ref[:, :gate_rem] = _sigmoid(acc[:, :gate_rem]).astype(o_ref.dtype)
        o_ref[:, gate_rem:] = acc[:, gate_rem:].astype(o_ref.dtype)

    @pl.when(j == uv_tile)
    def _():
        acc = tile()
        o_ref[:, :uv_off] = acc[:, :uv_off].astype(o_ref.dtype)
        o_ref[:, uv_off:] = jax.nn.gelu(acc[:, uv_off:]).astype(o_ref.dtype)

    @pl.when(j == uv_tile + 1)
    def _():
        acc = tile()
        o_ref[:, :uv_end] = jax.nn.gelu(acc[:, :uv_end]).astype(o_ref.dtype)
        o_ref[:, uv_end:] = acc[:, uv_end:].astype(o_ref.dtype)


def _in_proj(x2, g_mix, w_in_r, later_weights):
    t = x2.shape[0]
    ni, nj = t // IN_TM, IN_COLS // IN_TN
    plan = _cast_plan(ni * nj)

    def side_spec(rows, cols, br, start):
        nblk = rows // br
        return pl.BlockSpec((br, cols), lambda i, j: (jnp.clip(i * nj + j - start, 0, nblk - 1), 0))

    side_specs = [side_spec(rows, cols, br, start) for _, rows, cols, br, start in plan]
    outs = pl.pallas_call(
        functools.partial(_in_proj_kernel, plan),
        grid=(ni, nj),
        in_specs=[
            pl.BlockSpec((IN_TM, D_MODEL), lambda i, j: (i, 0)),
            pl.BlockSpec((1, D_MODEL), lambda i, j: (0, 0)),
            pl.BlockSpec((D_MODEL, IN_TN), lambda i, j: (0, j)),
        ] + side_specs,
        out_specs=[pl.BlockSpec((IN_TM, IN_TN), lambda i, j: (i, j))] + side_specs,
        out_shape=[jax.ShapeDtypeStruct((t, IN_COLS), BF16)]
        + [jax.ShapeDtypeStruct((rows, cols), BF16) for _, rows, cols, _, _ in plan],
        scratch_shapes=[pltpu.VMEM((IN_TM, D_MODEL), BF16)],
        compiler_params=pltpu.CompilerParams(
            dimension_semantics=("arbitrary", "arbitrary"),
            vmem_limit_bytes=VMEM_LIMIT_BYTES),
        name="in_proj",
    )(x2, g_mix, w_in_r, *[later_weights[name] for name, *_ in plan])
    return outs[0], {name: o for (name, *_), o in zip(plan, outs[1:])}


def _rope_tables(pos, invf):
    ang = pos * invf
    cos = jnp.cos(ang)
    sin = jnp.sin(ang)
    lane = lax.broadcasted_iota(jnp.int32, ang.shape, 1) % B_HEAD_DIM
    s_lo = jnp.where(lane < ROPE_HALF, -sin, 0.0)
    s_hi = jnp.where(lane >= ROPE_HALF, sin, 0.0)
    return cos, s_lo, s_hi


def _rope(z, tabs):
    cos, s_lo, s_hi = tabs
    return (z * cos + pltpu.roll(z, LANES - ROPE_HALF, 1) * s_lo
            + pltpu.roll(z, ROPE_HALF, 1) * s_hi)


def _head_norm(z, bd, g):
    ssq = jnp.dot((z * z).astype(BF16), bd, preferred_element_type=F32)
    return z * lax.rsqrt(ssq * (1.0 / B_HEAD_DIM) + EPS) * g


def _mixer_kernel(sinks_ref, x_ref, g0_ref, g1_ref, g2_ref, qb_ref, uv_ref, qc_ref,
                  kv_ref, pos_ref, kct_ref, vc_ref,
                  gav_ref, wsp_ref, bsp_ref, gq_ref, gk_ref, invf_ref, gcq_ref, bd_ref,
                  wa_ref, wb_ref, wc_ref, wo_ref, o_ref, y_ref, kprev_ref, vprev_ref):
    tm = x_ref.shape[0]
    n_chunks = tm // CHUNK
    tile = pl.program_id(1)
    bd = bd_ref[...]
    bd_kv = bd[:B_KV_WIDTH, :B_KV_WIDTH]

    @pl.when(tile == 0)
    def _():
        kprev_ref[...] = jnp.zeros_like(kprev_ref)
        vprev_ref[...] = jnp.zeros_like(vprev_ref)

    u = uv_ref[:, :A_WIDTH].astype(F32)
    v = _rms(uv_ref[:, A_WIDTH:].astype(F32), gav_ref[...]).astype(BF16)
    row = lax.broadcasted_iota(jnp.int32, (CHUNK, CHUNK), 0)
    col = lax.broadcasted_iota(jnp.int32, (CHUNK, CHUNK), 1)
    w_sp = [jnp.where(row >= col, wsp_ref[g], 0.0).astype(BF16) for g in range(A_GROUPS)]
    for c in range(n_chunks):
        rows = slice(c * CHUNK, (c + 1) * CHUNK)
        s = jnp.concatenate(
            [jnp.dot(w_sp[g], v[rows, g * A_GROUP_CH:(g + 1) * A_GROUP_CH],
                     preferred_element_type=F32) for g in range(A_GROUPS)], axis=1)
        y_ref[rows, 0:A_WIDTH] = (u[rows] * (s + bsp_ref[...])).astype(BF16)

    tabs = _rope_tables(pos_ref[...], invf_ref[...])
    k_cur = _rope(_head_norm(kv_ref[:, :B_KV_WIDTH].astype(F32), bd_kv, gk_ref[...]), tabs)
    kt_all = jnp.concatenate([kprev_ref[...], k_cur], axis=0).T.astype(BF16)
    v_all = jnp.concatenate([vprev_ref[...], kv_ref[:, B_KV_WIDTH:]], axis=0)
    kprev_ref[...] = k_cur[tm - WINDOW:]
    vprev_ref[...] = kv_ref[tm - WINDOW:, B_KV_WIDTH:]

    qb = qb_ref[...].astype(F32)
    scale = B_HEAD_DIM ** -0.5
    q_slabs = []
    for j in range(B_WIDTH // MXU_DIM):
        z = _head_norm(qb[:, j * MXU_DIM:(j + 1) * MXU_DIM], bd,
                       gq_ref[:, j * MXU_DIM:(j + 1) * MXU_DIM])
        for half in range(MXU_DIM // LANES):
            zz = _rope(z[:, half * LANES:(half + 1) * LANES], tabs) * scale
            q_slabs.append(zz.astype(BF16))

    upper = col > row
    lane_lo = col < B_HEAD_DIM
    spk = B_REP // 2
    zk = jnp.zeros((B_HEAD_DIM, 2 * WINDOW), BF16)
    zv = jnp.zeros((2 * WINDOW, B_HEAD_DIM), BF16)
    ov = jnp.ones((2 * WINDOW, B_HEAD_DIM), BF16)
    for c in range(n_chunks):
        rows = slice(c * CHUNK, (c + 1) * CHUNK)
        for h in range(B_KV_HEADS):
            kt = kt_all[h * B_HEAD_DIM:(h + 1) * B_HEAD_DIM, c * CHUNK:c * CHUNK + 2 * WINDOW]
            v2 = v_all[c * CHUNK:c * CHUNK + 2 * WINDOW, h * B_HEAD_DIM:(h + 1) * B_HEAD_DIM]
            k_bd = jnp.concatenate([jnp.concatenate([kt, zk], axis=1),
                                    jnp.concatenate([zk, kt], axis=1)], axis=0)
            v_bd = jnp.concatenate([jnp.concatenate([v2, zv, ov, zv], axis=1),
                                    jnp.concatenate([zv, v2, zv, ov], axis=1)], axis=0)
            q_stack = jnp.concatenate([q_slabs[h * spk + j][rows] for j in range(spk)], axis=0)
            s_all = jnp.dot(q_stack, k_bd, preferred_element_type=F32)
            p_rows, e_rows = [], []
            for j in range(spk):
                p_pair, e_pair = [], []
                for half in range(2):
                    blk = s_all[j * CHUNK:(j + 1) * CHUNK,
                                half * 2 * WINDOW:(half + 1) * 2 * WINDOW]
                    s_prev = blk[:, :WINDOW]
                    if c == 0:
                        s_prev = jnp.where(tile > 0, s_prev, -jnp.inf)
                    s = jnp.where(upper, s_prev, blk[:, WINDOW:])
                    sink = sinks_ref[h * B_REP + 2 * j + half]
                    m = jnp.maximum(jnp.max(s, axis=-1, keepdims=True), sink)
                    p = jnp.exp(s - m)
                    e_pair.append(jnp.exp(sink - m))
                    p_pair += [jnp.where(upper, p, 0.0), jnp.where(upper, 0.0, p)]
                p_rows.append(jnp.concatenate(p_pair, axis=1).astype(BF16))
                e_rows.append(jnp.where(lane_lo, e_pair[0], e_pair[1]))
            o_all = jnp.dot(jnp.concatenate(p_rows, axis=0), v_bd,
                            preferred_element_type=F32)
            for j in range(spk):
                o = o_all[j * CHUNK:(j + 1) * CHUNK]
                c0 = A_WIDTH + (h * spk + j) * LANES
                y_ref[rows, c0:c0 + LANES] = (o[:, :LANES] / (o[:, LANES:] + e_rows[j])).astype(BF16)

    qc = qc_ref[...].astype(F32)
    c_scale = C_HEAD_DIM ** -0.5
    for h in range(C_HEADS):
        cols = slice(h * C_HEAD_DIM, (h + 1) * C_HEAD_DIM)
        qn = _rms(qc[:, cols], gcq_ref[...]).astype(BF16)
        s = jnp.dot(qn, kct_ref[0, cols, :], preferred_element_type=F32) * c_scale
        m = jnp.max(s, axis=-1, keepdims=True)
        p = jnp.exp(s - m)
        l = jnp.sum(p, axis=-1, keepdims=True)
        o = jnp.dot(p.astype(BF16), vc_ref[0, :, cols], preferred_element_type=F32) / l
        c0 = A_WIDTH + B_WIDTH + h * C_HEAD_DIM
        y_ref[:, c0:c0 + C_HEAD_DIM] = o.astype(BF16)

    za = jnp.dot(y_ref[:, 0:A_WIDTH], wa_ref[...], preferred_element_type=F32)
    merged = g0_ref[...].astype(F32) * za
    zb = jnp.dot(y_ref[:, A_WIDTH:A_WIDTH + B_WIDTH], wb_ref[...], preferred_element_type=F32)
    merged = merged + g1_ref[...].astype(F32) * zb
    zc = jnp.dot(y_ref[:, A_WIDTH + B_WIDTH:], wc_ref[...], preferred_element_type=F32)
    merged = merged + g2_ref[...].astype(F32) * zc
    o_ref[...] = x_ref[...] + jnp.dot(merged.astype(BF16), wo_ref[...],
                                      preferred_element_type=F32)


def _mixers(x2, proj, pos2, kct, vc, sinks, g_a_v, w_sp, bias_sp, gq_t, gk_t, invf, gcq,
            bd, wa, wb, wc, wo, bn, s_len):
    t = x2.shape[0]
    tm = MIX_TM
    tps = s_len // tm

    def row(b, i):
        return b * tps + i

    def const(*idx):
        return lambda b, i: idx

    in_specs = [
        pl.BlockSpec(memory_space=pltpu.SMEM),
        pl.BlockSpec((tm, D_MODEL), lambda b, i: (row(b, i), 0)),
        pl.BlockSpec((tm, D_MODEL), lambda b, i: (row(b, i), 0)),
        pl.BlockSpec((tm, D_MODEL), lambda b, i: (row(b, i), 1)),
        pl.BlockSpec((tm, D_MODEL), lambda b, i: (row(b, i), 2)),
        pl.BlockSpec((tm, B_WIDTH), lambda b, i: (row(b, i), COL_QB // B_WIDTH)),
        pl.BlockSpec((tm, 2 * A_WIDTH), lambda b, i: (row(b, i), COL_UV // (2 * A_WIDTH))),
        pl.BlockSpec((tm, C_WIDTH), lambda b, i: (row(b, i), COL_QC // C_WIDTH)),
        pl.BlockSpec((tm, 2 * B_KV_WIDTH), lambda b, i: (row(b, i), COL_KV // (2 * B_KV_WIDTH))),
        pl.BlockSpec((tm, 1), lambda b, i: (row(b, i), 0)),
        pl.BlockSpec((1, C_WIDTH, MEM_LEN), lambda b, i: (b, 0, 0)),
        pl.BlockSpec((1, MEM_LEN, C_WIDTH), lambda b, i: (b, 0, 0)),
        pl.BlockSpec((1, A_WIDTH), const(0, 0)),
        pl.BlockSpec((A_GROUPS, CHUNK, CHUNK), const(0, 0, 0)),
        pl.BlockSpec((CHUNK, A_WIDTH), const(0, 0)),
        pl.BlockSpec((1, B_WIDTH), const(0, 0)),
        pl.BlockSpec((1, B_KV_WIDTH), const(0, 0)),
        pl.BlockSpec((1, LANES), const(0, 0)),
        pl.BlockSpec((1, C_HEAD_DIM), const(0, 0)),
        pl.BlockSpec((MXU_DIM, MXU_DIM), const(0, 0)),
        pl.BlockSpec((A_WIDTH, D_MODEL), const(0, 0), pipeline_mode=pl.Buffered(1)),
        pl.BlockSpec((B_WIDTH, D_MODEL), const(0, 0), pipeline_mode=pl.Buffered(1)),
        pl.BlockSpec((C_WIDTH, D_MODEL), const(0, 0), pipeline_mode=pl.Buffered(1)),
        pl.BlockSpec((D_MODEL, D_MODEL), const(0, 0), pipeline_mode=pl.Buffered(1)),
    ]
    return pl.pallas_call(
        _mixer_kernel,
        grid=(bn, tps),
        in_specs=in_specs,
        out_specs=pl.BlockSpec((tm, D_MODEL), lambda b, i: (row(b, i), 0)),
        out_shape=jax.ShapeDtypeStruct((t, D_MODEL), F32),
        scratch_shapes=[
            pltpu.VMEM((tm, D_MODEL), BF16),
            pltpu.VMEM((WINDOW, B_KV_WIDTH), F32),
            pltpu.VMEM((WINDOW, B_KV_WIDTH), BF16),
        ],
        compiler_params=pltpu.CompilerParams(
            dimension_semantics=("arbitrary", "arbitrary"),
            vmem_limit_bytes=VMEM_LIMIT_BYTES),
        name="mixers",
    )(sinks, x2, proj, proj, proj, proj, proj, proj, proj, pos2, kct, vc,
      g_a_v, w_sp, bias_sp, gq_t, gk_t, invf, gcq, bd, wa, wb, wc, wo)


def _ffn_kernel(tiles_per_seq, x_ref, g_ref, wua_ref, wub_ref, cwa_ref, cwb_ref, cba_ref,
                cbb_ref, wd_ref, o_ref, h_ref, tail_ref, ubuf_ref):
    i = pl.program_id(0)
    f = pl.program_id(1)
    nf = pl.num_programs(1) - 1
    tm = x_ref.shape[0]
    fc = wd_ref.shape[0]
    ppw = fc // MXU_DIM
    n_up = 2 * ppw
    n_gate = tm // CHUNK
    gpu = n_gate // n_up

    def up_piece(k):
        slot = f % 2
        w_ref = wua_ref if k < ppw else wub_ref
        wcols = slice((k % ppw) * MXU_DIM, (k % ppw + 1) * MXU_DIM)
        cols = slice(k * MXU_DIM, (k + 1) * MXU_DIM)
        up = jnp.dot(h_ref[...], w_ref[:, wcols], preferred_element_type=F32)
        seq_start = i % tiles_per_seq == 0
        ubuf_ref[slot, 0:SUBLANES, cols] = jnp.where(seq_start, 0.0, tail_ref[f, :, cols])
        ubuf_ref[slot, SUBLANES:, cols] = up
        tail_ref[f, :, cols] = up[tm - SUBLANES:]

    def gate_piece(k):
        slot = (f + 1) % 2
        r0 = SUBLANES + k * CHUNK
        cw = jnp.concatenate([cwa_ref[...], cwb_ref[...]], axis=1)
        cb = jnp.concatenate([cba_ref[...], cbb_ref[...]], axis=1)
        conv = (cw[0:1] * ubuf_ref[slot, r0 - 2:r0 - 2 + CHUNK, :]
                + cw[1:2] * ubuf_ref[slot, r0 - 1:r0 - 1 + CHUNK, :]
                + cw[2:3] * ubuf_ref[slot, r0:r0 + CHUNK, :] + cb)
        ha = 0.5 * conv[:, :fc]
        return ((ha + ha * jnp.tanh(ha)) * conv[:, fc:]).astype(BF16)

    @pl.when(f == 0)
    def _():
        x = x_ref[...]
        h_ref[...] = _rms(x, g_ref[...]).astype(BF16)
        o_ref[...] = x

        @pl.when(i == 0)
        def _():
            tail_ref[...] = jnp.zeros_like(tail_ref)

        for k in range(n_up):
            up_piece(k)

    @pl.when((f > 0) & (f < nf))
    def _():
        acts = []
        for k in range(n_up):
            up_piece(k)
            acts += [gate_piece(gpu * k + g) for g in range(gpu)]
        act = jnp.concatenate(acts, axis=0)
        o_ref[...] += jnp.dot(act, wd_ref[...], preferred_element_type=F32)

    @pl.when(f == nf)
    def _():
        act = jnp.concatenate([gate_piece(k) for k in range(n_gate)], axis=0)
        o_ref[...] += jnp.dot(act, wd_ref[...], preferred_element_type=F32)


def _conv_ffn(x2, g_ffn, w_up, conv_w, conv_b, w_down, s_len):
    t = x2.shape[0]
    tm, fc = FFN_TM, FFN_FC
    nf = D_FF // fc

    def up_idx(f):
        return jnp.minimum(f, nf - 1)

    def down_idx(f):
        return jnp.maximum(f - 1, 0)

    return pl.pallas_call(
        functools.partial(_ffn_kernel, s_len // tm),
        grid=(t // tm, nf + 1),
        in_specs=[
            pl.BlockSpec((tm, D_MODEL), lambda i, f: (i, 0), pipeline_mode=pl.Buffered(1)),
            pl.BlockSpec((1, D_MODEL), lambda i, f: (0, 0)),
            pl.BlockSpec((D_MODEL, fc), lambda i, f: (0, up_idx(f))),
            pl.BlockSpec((D_MODEL, fc), lambda i, f: (0, nf + up_idx(f))),
            pl.BlockSpec((CONV_W, fc), lambda i, f: (0, down_idx(f))),
            pl.BlockSpec((CONV_W, fc), lambda i, f: (0, nf + down_idx(f))),
            pl.BlockSpec((1, fc), lambda i, f: (0, down_idx(f))),
            pl.BlockSpec((1, fc), lambda i, f: (0, nf + down_idx(f))),
            pl.BlockSpec((fc, D_MODEL), lambda i, f: (down_idx(f), 0)),
        ],
        out_specs=pl.BlockSpec((tm, D_MODEL), lambda i, f: (i, 0)),
        out_shape=jax.ShapeDtypeStruct((t, D_MODEL), F32),
        scratch_shapes=[
            pltpu.VMEM((tm, D_MODEL), BF16),
            pltpu.VMEM((nf, SUBLANES, 2 * fc), F32),
            pltpu.VMEM((2, tm + SUBLANES, 2 * fc), F32),
        ],
        compiler_params=pltpu.CompilerParams(
            dimension_semantics=("arbitrary", "arbitrary"),
            vmem_limit_bytes=VMEM_LIMIT_BYTES),
        name="conv_ffn",
    )(x2, g_ffn, w_up, w_up, conv_w, conv_w, conv_b, conv_b, w_down)


def _reorder_in_cols(w_in):
    o = np.cumsum([0, A_WIDTH, A_WIDTH, B_WIDTH, B_KV_WIDTH, B_KV_WIDTH, C_WIDTH]).tolist()
    u_a, v_a, q_b, k_b, v_b, q_c = (w_in[:, o[n]:o[n + 1]] for n in range(6))
    return jnp.concatenate([w_in[:, MAIN_COLS:], q_b, u_a, v_a, q_c, k_b, v_b], axis=1)


def kernel(x, mem, positions, g_mix, w_in, g_a_v, w_spatial, b_spatial, g_b_q, g_b_k, sinks,
           g_mem, w_mem_kv, g_c_q, g_c_k, w_branch_a, w_branch_b, w_branch_c, w_out, g_ffn,
           w_up, conv_w, conv_b, w_down):
    bn, s_len, _ = x.shape
    depth = w_in.shape[0]
    assert s_len % MIX_TM == 0 and s_len % FFN_TM == 0 and (bn * s_len) % IN_TM == 0
    t = bn * s_len
    x2 = x.reshape(t, D_MODEL)
    pos2 = positions.astype(F32).reshape(t, 1)

    inv = ROPE_THETA ** (-jnp.arange(ROPE_HALF, dtype=F32) / ROPE_HALF)
    inv_head = jnp.concatenate([inv, inv, jnp.zeros((B_HEAD_DIM - ROPE_DIM,), F32)])
    invf = jnp.tile(inv_head, LANES // B_HEAD_DIM).reshape(1, LANES)
    grp = np.arange(MXU_DIM) // B_HEAD_DIM
    bd = jnp.asarray(grp[:, None] == grp[None, :], dtype=BF16)

    for l in range(depth):
        kct, vc = _mem_kv(mem, g_mem[l].reshape(1, D_MODEL), w_mem_kv[l].astype(BF16),
                          g_c_k[l].reshape(1, C_HEAD_DIM))
        proj, wb16 = _in_proj(
            x2, g_mix[l].reshape(1, D_MODEL), _reorder_in_cols(w_in[l]).astype(BF16),
            dict(w_up=w_up[l], w_down=w_down[l], w_out=w_out[l], w_branch_a=w_branch_a[l],
                 w_branch_b=w_branch_b[l], w_branch_c=w_branch_c[l]))
        bias_sp = jnp.repeat(b_spatial[l].T, A_GROUP_CH, axis=1)
        x2 = _mixers(
            x2, proj, pos2, kct, vc, sinks[l], g_a_v[l].reshape(1, A_WIDTH), w_spatial[l],
            bias_sp, jnp.tile(g_b_q[l], B_HEADS).reshape(1, B_WIDTH),
            jnp.tile(g_b_k[l], B_KV_HEADS).reshape(1, B_KV_WIDTH), invf,
            g_c_q[l].reshape(1, C_HEAD_DIM), bd,
            wb16["w_branch_a"], wb16["w_branch_b"], wb16["w_branch_c"], wb16["w_out"],
            bn, s_len)
        x2 = _conv_ffn(x2, g_ffn[l].reshape(1, D_MODEL), wb16["w_up"], conv_w[l],
                       conv_b[l].reshape(1, 2 * D_FF), wb16["w_down"], s_len)
    return x2.reshape(bn, s_len, D_MODEL)
```

```python
import functools

import jax
import jax.numpy as jnp
import numpy as np
from jax import lax
from jax.experimental import pallas as pl
from jax.experimental.pallas import tpu as pltpu

F32 = jnp.float32
BF16 = jnp.bfloat16

D_MODEL = 2048
MEM_LEN = 256
EPS = 1e-6
CHUNK = 128
A_GROUPS = 4
A_GROUP_CH = 128
A_WIDTH = A_GROUPS * A_GROUP_CH
WINDOW = 128
B_HEADS = 16
B_KV_HEADS = 2
B_REP = B_HEADS // B_KV_HEADS
B_HEAD_DIM = 64
B_WIDTH = B_HEADS * B_HEAD_DIM
B_KV_WIDTH = B_KV_HEADS * B_HEAD_DIM
ROPE_DIM = B_HEAD_DIM // 4
ROPE_HALF = ROPE_DIM // 2
ROPE_THETA = 500000.0
C_HEADS = 4
C_HEAD_DIM = 128
C_WIDTH = C_HEADS * C_HEAD_DIM
N_BRANCH = 3
D_FF = 5632
CONV_W = 3
MAIN_COLS = 2 * A_WIDTH + B_WIDTH + 2 * B_KV_WIDTH + C_WIDTH
IN_COLS = MAIN_COLS + N_BRANCH * D_MODEL

LANES = 128
SUBLANES = 8
MXU_DIM = 256
VMEM_LIMIT_BYTES = 56 * 1024 * 1024

IN_TM = 1024
IN_TN = 1280
MIX_TM = 256
FFN_TM = 1024
FFN_FC = 512

COL_QB = N_BRANCH * D_MODEL
COL_UV = COL_QB + B_WIDTH
COL_QC = COL_UV + 2 * A_WIDTH
COL_KV = COL_QC + C_WIDTH


def _sigmoid(x):
    return 0.5 * jnp.tanh(0.5 * x) + 0.5


def _rms(x, g):
    ms = jnp.mean(x * x, axis=-1, keepdims=True)
    return x * lax.rsqrt(ms + EPS) * g


def _mem_kv_kernel(mem_ref, g_ref, w_ref, gk_ref, kt_ref, v_ref):
    mh = _rms(mem_ref[0], g_ref[...]).astype(BF16)
    kv = jnp.dot(mh, w_ref[...], preferred_element_type=F32)
    for h in range(C_HEADS):
        k = _rms(kv[:, h * C_HEAD_DIM:(h + 1) * C_HEAD_DIM], gk_ref[...])
        kt_ref[0, h * C_HEAD_DIM:(h + 1) * C_HEAD_DIM, :] = k.T.astype(BF16)
    v_ref[0] = kv[:, C_WIDTH:].astype(BF16)


def _mem_kv(mem, g_mem, w_kv, g_c_k):
    bn = mem.shape[0]
    return pl.pallas_call(
        _mem_kv_kernel,
        grid=(bn,),
        in_specs=[
            pl.BlockSpec((1, MEM_LEN, D_MODEL), lambda b: (b, 0, 0)),
            pl.BlockSpec((1, D_MODEL), lambda b: (0, 0)),
            pl.BlockSpec((D_MODEL, 2 * C_WIDTH), lambda b: (0, 0)),
            pl.BlockSpec((1, C_HEAD_DIM), lambda b: (0, 0)),
        ],
        out_specs=[
            pl.BlockSpec((1, C_WIDTH, MEM_LEN), lambda b: (b, 0, 0)),
            pl.BlockSpec((1, MEM_LEN, C_WIDTH), lambda b: (b, 0, 0)),
        ],
        out_shape=[
            jax.ShapeDtypeStruct((bn, C_WIDTH, MEM_LEN), BF16),
            jax.ShapeDtypeStruct((bn, MEM_LEN, C_WIDTH), BF16),
        ],
        compiler_params=pltpu.CompilerParams(
            dimension_semantics=("arbitrary",), vmem_limit_bytes=VMEM_LIMIT_BYTES),
        name="mem_kv",
    )(mem, g_mem, w_kv, g_c_k)


def _cast_plan(n_steps):
    plan = [
        ("w_up", D_MODEL, 2 * D_FF, 32, 0),
        ("w_down", D_FF, D_MODEL, 128, 64),
        ("w_out", D_MODEL, D_MODEL, 64, 0),
        ("w_branch_b", B_WIDTH, D_MODEL, 64, 32),
        ("w_branch_a", A_WIDTH, D_MODEL, 64, 48),
        ("w_branch_c", C_WIDTH, D_MODEL, 64, 56),
    ]
    for _, rows, _, br, start in plan:
        assert rows % br == 0 and start + rows // br <= n_steps
    return plan


def _in_proj_kernel(plan, x_ref, g_ref, w_ref, *refs):
    n_side = len(plan)
    side_in, o_ref, side_out, h_ref = (refs[:n_side], refs[n_side],
                                       refs[n_side + 1:2 * n_side + 1], refs[-1])
    j = pl.program_id(1)
    step = pl.program_id(0) * pl.num_programs(1) + j

    for (_, rows, _, br, start), src, dst in zip(plan, side_in, side_out):
        @pl.when((step >= start) & (step < start + rows // br))
        def _():
            dst[...] = src[...].astype(BF16)

    @pl.when(j == 0)
    def _():
        h_ref[...] = _rms(x_ref[...], g_ref[...]).astype(BF16)

    def tile():
        return jnp.dot(h_ref[...], w_ref[...], preferred_element_type=F32)

    gate_tiles, gate_rem = divmod(COL_QB, IN_TN)
    uv_tile, uv_off = divmod(COL_UV, IN_TN)
    uv_end = COL_QC - (uv_tile + 1) * IN_TN
    assert gate_rem > 0 and uv_tile == gate_tiles + 1 and 0 < uv_end < IN_TN
    assert uv_tile + 2 == IN_COLS // IN_TN

    @pl.when(j < gate_tiles)
    def _():
        o_ref[...] = _sigmoid(tile()).astype(o_ref.dtype)

    @pl.when(j == gate_tiles)
    def _():
        acc = tile()
        o_ref[:, :gate_rem] = _sigmoid(acc[:, :gate_rem]).astype(o_ref.dtype)
        o_ref[:, gate_rem:] = acc[:, gate_rem:].astype(o_ref.dtype)

    @pl.when(j == uv_tile)
    def _():
        acc = tile()
        o_ref[:, :uv_off] = acc[:, :uv_off].astype(o_ref.dtype)
        o_ref[:, uv_off:] = jax.nn.gelu(acc[:, uv_off:]).astype(o_ref.dtype)

    @pl.when(j == uv_tile + 1)
    def _():
        acc = tile()
        o_ref[:, :uv_end] = jax.nn.gelu(acc[:, :uv_end]).astype(o_ref.dtype)
        o_ref[:, uv_end:] = acc[:, uv_end:].astype(o_ref.dtype)


def _in_proj(x2, g_mix, w_in_r, later_weights):
    t = x2.shape[0]
    ni, nj = t // IN_TM, IN_COLS // IN_TN
    plan = _cast_plan(ni * nj)

    def side_spec(rows, cols, br, start):
        nblk = rows // br
        return pl.BlockSpec((br, cols), lambda i, j: (jnp.clip(i * nj + j - start, 0, nblk - 1), 0))

    side_specs = [side_spec(rows, cols, br, start) for _, rows, cols, br, start in plan]
    outs = pl.pallas_call(
        functools.partial(_in_proj_kernel, plan),
        grid=(ni, nj),
        in_specs=[
            pl.BlockSpec((IN_TM, D_MODEL), lambda i, j: (i, 0)),
            pl.BlockSpec((1, D_MODEL), lambda i, j: (0, 0)),
            pl.BlockSpec((D_MODEL, IN_TN), lambda i, j: (0, j)),
        ] + side_specs,
        out_specs=[pl.BlockSpec((IN_TM, IN_TN), lambda i, j: (i, j))] + side_specs,
        out_shape=[jax.ShapeDtypeStruct((t, IN_COLS), BF16)]
        + [jax.ShapeDtypeStruct((rows, cols), BF16) for _, rows, cols, _, _ in plan],
        scratch_shapes=[pltpu.VMEM((IN_TM, D_MODEL), BF16)],
        compiler_params=pltpu.CompilerParams(
            dimension_semantics=("arbitrary", "arbitrary"),
            vmem_limit_bytes=VMEM_LIMIT_BYTES),
        name="in_proj",
    )(x2, g_mix, w_in_r, *[later_weights[name] for name, *_ in plan])
    return outs[0], {name: o for (name, *_), o in zip(plan, outs[1:])}


def _rope_tables(pos, invf):
    ang = pos * invf
    cos = jnp.cos(ang)
    sin = jnp.sin(ang)
    lane = lax.broadcasted_iota(jnp.int32, ang.shape, 1) % B_HEAD_DIM
    s_lo = jnp.where(lane < ROPE_HALF, -sin, 0.0)
    s_hi = jnp.where(lane >= ROPE_HALF, sin, 0.0)
    return cos, s_lo, s_hi


def _rope(z, tabs):
    cos, s_lo, s_hi = tabs
    return (z * cos + pltpu.roll(z, LANES - ROPE_HALF, 1) * s_lo
            + pltpu.roll(z, ROPE_HALF, 1) * s_hi)


def _head_norm(z, bd, g):
    ssq = jnp.dot((z * z).astype(BF16), bd, preferred_element_type=F32)
    return z * lax.rsqrt(ssq * (1.0 / B_HEAD_DIM) + EPS) * g


def _mixer_kernel(tps, sinks_ref, x_ref, g0_ref, g1_ref, g2_ref, qb_ref, uv_ref, qc_ref,
                  kv_ref, pos_ref, kct_ref, vc_ref,
                  gav_ref, wsp_ref, bsp_ref, gq_ref, gk_ref, invf_ref, gcq_ref, bd_ref,
                  wa_ref, wb_ref, wc_ref, wo_ref, o_ref, y0_ref, y1_ref, mg_ref,
                  kprev_ref, vprev_ref):
    step = pl.program_id(0)
    tile = jnp.minimum(step, pl.num_programs(0) - 2) % tps
    tm = x_ref.shape[0]
    n_chunks = tm // CHUNK

    @pl.when(step == 0)
    def _():
        y1_ref[...] = jnp.zeros_like(y1_ref)

    @pl.when(tile == 0)
    def _():
        kprev_ref[...] = jnp.zeros_like(kprev_ref)
        vprev_ref[...] = jnp.zeros_like(vprev_ref)

    def body(y_ref, yp_ref):
        att, tail = [], []
        st = {}
        bd = bd_ref[...]
        bd_kv = bd[:B_KV_WIDTH, :B_KV_WIDTH]
        row = lax.broadcasted_iota(jnp.int32, (CHUNK, CHUNK), 0)
        col = lax.broadcasted_iota(jnp.int32, (CHUNK, CHUNK), 1)

        def a_chunk(c):
            rows = slice(c * CHUNK, (c + 1) * CHUNK)
            u = uv_ref[rows, :A_WIDTH].astype(F32)
            v = _rms(uv_ref[rows, A_WIDTH:].astype(F32), gav_ref[...]).astype(BF16)
            s_ = jnp.concatenate(
                [jnp.dot(jnp.where(row >= col, wsp_ref[g], 0.0).astype(BF16),
                         v[:, g * A_GROUP_CH:(g + 1) * A_GROUP_CH],
                         preferred_element_type=F32) for g in range(A_GROUPS)], axis=1)
            y_ref[rows, 0:A_WIDTH] = (u * (s_ + bsp_ref[...])).astype(BF16)
        att += [functools.partial(a_chunk, c) for c in range(n_chunks)]

        def b_keys():
            tabs = _rope_tables(pos_ref[...], invf_ref[...])
            k_cur = _rope(_head_norm(kv_ref[:, :B_KV_WIDTH].astype(F32), bd_kv, gk_ref[...]), tabs)
            st["tabs"] = tabs
            st["kt_all"] = jnp.concatenate([kprev_ref[...], k_cur], axis=0).T.astype(BF16)
            st["v_all"] = jnp.concatenate([vprev_ref[...], kv_ref[:, B_KV_WIDTH:]], axis=0)
            kprev_ref[...] = k_cur[tm - WINDOW:]
            vprev_ref[...] = kv_ref[tm - WINDOW:, B_KV_WIDTH:]
            st["q"] = []
        att.append(b_keys)

        scale = B_HEAD_DIM ** -0.5

        def b_q(j):
            z = _head_norm(qb_ref[:, j * MXU_DIM:(j + 1) * MXU_DIM].astype(F32), bd,
                           gq_ref[:, j * MXU_DIM:(j + 1) * MXU_DIM])
            for half in range(MXU_DIM // LANES):
                zz = _rope(z[:, half * LANES:(half + 1) * LANES], st["tabs"]) * scale
                st["q"].append(zz.astype(BF16))
        att += [functools.partial(b_q, j) for j in range(B_WIDTH // MXU_DIM)]

        upper = col > row
        lane_lo = col < B_HEAD_DIM
        spk = B_REP // 2

        def b_attend(c, h):
            zk = jnp.zeros((B_HEAD_DIM, 2 * WINDOW), BF16)
            zv = jnp.zeros((2 * WINDOW, B_HEAD_DIM), BF16)
            ov = jnp.ones((2 * WINDOW, B_HEAD_DIM), BF16)
            rows = slice(c * CHUNK, (c + 1) * CHUNK)
            kt = st["kt_all"][h * B_HEAD_DIM:(h + 1) * B_HEAD_DIM, c * CHUNK:c * CHUNK + 2 * WINDOW]
            v2 = st["v_all"][c * CHUNK:c * CHUNK + 2 * WINDOW, h * B_HEAD_DIM:(h + 1) * B_HEAD_DIM]
            k_bd = jnp.concatenate([jnp.concatenate([kt, zk], axis=1),
                                    jnp.concatenate([zk, kt], axis=1)], axis=0)
            v_bd = jnp.concatenate([jnp.concatenate([v2, zv, ov, zv], axis=1),
                                    jnp.concatenate([zv, v2, zv, ov], axis=1)], axis=0)
            q_stack = jnp.concatenate([st["q"][h * spk + j][rows] for j in range(spk)], axis=0)
            s_all = jnp.dot(q_stack, k_bd, preferred_element_type=F32)
            p_rows, e_rows = [], []
            for j in range(spk):
                p_pair, e_pair = [], []
                for half in range(2):
                    blk = s_all[j * CHUNK:(j + 1) * CHUNK,
                                half * 2 * WINDOW:(half + 1) * 2 * WINDOW]
                    s_prev = blk[:, :WINDOW]
                    if c == 0:
                        s_prev = jnp.where(tile > 0, s_prev, -jnp.inf)
                    s_ = jnp.where(upper, s_prev, blk[:, WINDOW:])
                    sink = sinks_ref[h * B_REP + 2 * j + half]
                    m = jnp.maximum(jnp.max(s_, axis=-1, keepdims=True), sink)
                    p = jnp.exp(s_ - m)
                    e_pair.append(jnp.exp(sink - m))
                    p_pair += [jnp.where(upper, p, 0.0), jnp.where(upper, 0.0, p)]
                p_rows.append(jnp.concatenate(p_pair, axis=1).astype(BF16))
                e_rows.append(jnp.where(lane_lo, e_pair[0], e_pair[1]))
            o_all = jnp.dot(jnp.concatenate(p_rows, axis=0), v_bd,
                            preferred_element_type=F32)
            for j in range(spk):
                o = o_all[j * CHUNK:(j + 1) * CHUNK]
                c0 = A_WIDTH + (h * spk + j) * LANES
                y_ref[rows, c0:c0 + LANES] = (o[:, :LANES] / (o[:, LANES:] + e_rows[j])).astype(BF16)
        att += [functools.partial(b_attend, c, h) for c in range(n_chunks) for h in range(B_KV_HEADS)]

        c_scale = C_HEAD_DIM ** -0.5

        def c_head(h):
            cols = slice(h * C_HEAD_DIM, (h + 1) * C_HEAD_DIM)
            qn = _rms(qc_ref[:, cols].astype(F32), gcq_ref[...]).astype(BF16)
            s_ = jnp.dot(qn, kct_ref[0, cols, :], preferred_element_type=F32) * c_scale
            m = jnp.max(s_, axis=-1, keepdims=True)
            p = jnp.exp(s_ - m)
            l = jnp.sum(p, axis=-1, keepdims=True)
            o = jnp.dot(p.astype(BF16), vc_ref[0, :, cols], preferred_element_type=F32) / l
            c0 = A_WIDTH + B_WIDTH + h * C_HEAD_DIM
            y_ref[:, c0:c0 + C_HEAD_DIM] = o.astype(BF16)
        att += [functools.partial(c_head, h) for h in range(C_HEADS)]

        def merge_cols(cb):
            cols = slice(cb * MXU_DIM, (cb + 1) * MXU_DIM)
            za = jnp.dot(yp_ref[:, 0:A_WIDTH], wa_ref[:, cols], preferred_element_type=F32)
            zb = jnp.dot(yp_ref[:, A_WIDTH:A_WIDTH + B_WIDTH], wb_ref[:, cols],
                         preferred_element_type=F32)
            zc = jnp.dot(yp_ref[:, A_WIDTH + B_WIDTH:], wc_ref[:, cols],
                         preferred_element_type=F32)
            mg_ref[:, cols] = (g0_ref[:, cols].astype(F32) * za + g1_ref[:, cols].astype(F32) * zb
                               + g2_ref[:, cols].astype(F32) * zc).astype(BF16)

        def out_cols(cb):
            cols = slice(cb * MXU_DIM, (cb + 1) * MXU_DIM)
            o_ref[:, cols] = x_ref[:, cols] + jnp.dot(mg_ref[...], wo_ref[:, cols],
                                                      preferred_element_type=F32)
        n_cb = D_MODEL // MXU_DIM
        tail += [functools.partial(merge_cols, cb) for cb in range(n_cb)]
        tail += [functools.partial(out_cols, cb) for cb in range(n_cb)]

        n_a, n_t = len(att), len(tail)
        ti = 0
        for ai, piece in enumerate(att):
            piece()
            while ti < n_t and (ti + 1) * n_a <= (ai + 1) * n_t:
                tail[ti]()
                ti += 1
        while ti < n_t:
            tail[ti]()
            ti += 1

    for par, (y_ref, yp_ref) in enumerate(((y0_ref, y1_ref), (y1_ref, y0_ref))):
        @pl.when(step % 2 == par)
        def _():
            body(y_ref, yp_ref)


def _mixers(x2, proj, pos2, kct, vc, sinks, g_a_v, w_sp, bias_sp, gq_t, gk_t, invf, gcq,
            bd, wa, wb, wc, wo, bn, s_len):
    t = x2.shape[0]
    tm = MIX_TM
    tps = s_len // tm
    n_tiles = t // tm

    def cur(s):
        return jnp.minimum(s, n_tiles - 1)

    def prev(s):
        return jnp.maximum(s - 1, 0)

    def const(*idx):
        return lambda s: idx

    in_specs = [
        pl.BlockSpec(memory_space=pltpu.SMEM),
        pl.BlockSpec((tm, D_MODEL), lambda s: (prev(s), 0)),
        pl.BlockSpec((tm, D_MODEL), lambda s: (prev(s), 0)),
        pl.BlockSpec((tm, D_MODEL), lambda s: (prev(s), 1)),
        pl.BlockSpec((tm, D_MODEL), lambda s: (prev(s), 2)),
        pl.BlockSpec((tm, B_WIDTH), lambda s: (cur(s), COL_QB // B_WIDTH)),
        pl.BlockSpec((tm, 2 * A_WIDTH), lambda s: (cur(s), COL_UV // (2 * A_WIDTH))),
        pl.BlockSpec((tm, C_WIDTH), lambda s: (cur(s), COL_QC // C_WIDTH)),
        pl.BlockSpec((tm, 2 * B_KV_WIDTH), lambda s: (cur(s), COL_KV // (2 * B_KV_WIDTH))),
        pl.BlockSpec((tm, 1), lambda s: (cur(s), 0)),
        pl.BlockSpec((1, C_WIDTH, MEM_LEN), lambda s: (cur(s) // tps, 0, 0)),
        pl.BlockSpec((1, MEM_LEN, C_WIDTH), lambda s: (cur(s) // tps, 0, 0)),
        pl.BlockSpec((1, A_WIDTH), const(0, 0)),
        pl.BlockSpec((A_GROUPS, CHUNK, CHUNK), const(0, 0, 0)),
        pl.BlockSpec((CHUNK, A_WIDTH), const(0, 0)),
        pl.BlockSpec((1, B_WIDTH), const(0, 0)),
        pl.BlockSpec((1, B_KV_WIDTH), const(0, 0)),
        pl.BlockSpec((1, LANES), const(0, 0)),
        pl.BlockSpec((1, C_HEAD_DIM), const(0, 0)),
        pl.BlockSpec((MXU_DIM, MXU_DIM), const(0, 0)),
        pl.BlockSpec((A_WIDTH, D_MODEL), const(0, 0), pipeline_mode=pl.Buffered(1)),
        pl.BlockSpec((B_WIDTH, D_MODEL), const(0, 0), pipeline_mode=pl.Buffered(1)),
        pl.BlockSpec((C_WIDTH, D_MODEL), const(0, 0), pipeline_mode=pl.Buffered(1)),
        pl.BlockSpec((D_MODEL, D_MODEL), const(0, 0), pipeline_mode=pl.Buffered(1)),
    ]
    return pl.pallas_call(
        functools.partial(_mixer_kernel, tps),
        grid=(n_tiles + 1,),
        in_specs=in_specs,
        out_specs=pl.BlockSpec((tm, D_MODEL), lambda s: (prev(s), 0)),
        out_shape=jax.ShapeDtypeStruct((t, D_MODEL), F32),
        scratch_shapes=[
            pltpu.VMEM((tm, D_MODEL), BF16),
            pltpu.VMEM((tm, D_MODEL), BF16),
            pltpu.VMEM((tm, D_MODEL), BF16),
            pltpu.VMEM((WINDOW, B_KV_WIDTH), F32),
            pltpu.VMEM((WINDOW, B_KV_WIDTH), BF16),
        ],
        compiler_params=pltpu.CompilerParams(
            dimension_semantics=("arbitrary",),
            vmem_limit_bytes=VMEM_LIMIT_BYTES),
        name="mixers",
    )(sinks, x2, proj, proj, proj, proj, proj, proj, proj, pos2, kct, vc,
      g_a_v, w_sp, bias_sp, gq_t, gk_t, invf, gcq, bd, wa, wb, wc, wo)


def _ffn_kernel(tiles_per_seq, x_ref, g_ref, wua_ref, wub_ref, cwa_ref, cwb_ref, cba_ref,
                cbb_ref, wd_ref, o_ref, h_ref, tail_ref, ubuf_ref):
    i = pl.program_id(0)
    f = pl.program_id(1)
    nf = pl.num_programs(1) - 1
    tm = x_ref.shape[0]
    fc = wd_ref.shape[0]
    ppw = fc // MXU_DIM
    n_up = 2 * ppw
    n_gate = tm // CHUNK
    gpu = n_gate // n_up

    def up_piece(k):
        slot = f % 2
        w_ref = wua_ref if k < ppw else wub_ref
        wcols = slice((k % ppw) * MXU_DIM, (k % ppw + 1) * MXU_DIM)
        cols = slice(k * MXU_DIM, (k + 1) * MXU_DIM)
        up = jnp.dot(h_ref[...], w_ref[:, wcols], preferred_element_type=F32)
        seq_start = i % tiles_per_seq == 0
        ubuf_ref[slot, 0:SUBLANES, cols] = jnp.where(seq_start, 0.0, tail_ref[f, :, cols])
        ubuf_ref[slot, SUBLANES:, cols] = up
        tail_ref[f, :, cols] = up[tm - SUBLANES:]

    def gate_piece(k):
        slot = (f + 1) % 2
        r0 = SUBLANES + k * CHUNK
        cw = jnp.concatenate([cwa_ref[...], cwb_ref[...]], axis=1)
        cb = jnp.concatenate([cba_ref[...], cbb_ref[...]], axis=1)
        conv = (cw[0:1] * ubuf_ref[slot, r0 - 2:r0 - 2 + CHUNK, :]
                + cw[1:2] * ubuf_ref[slot, r0 - 1:r0 - 1 + CHUNK, :]
                + cw[2:3] * ubuf_ref[slot, r0:r0 + CHUNK, :] + cb)
        ha = 0.5 * conv[:, :fc]
        return ((ha + ha * jnp.tanh(ha)) * conv[:, fc:]).astype(BF16)

    @pl.when(f == 0)
    def _():
        x = x_ref[...]
        h_ref[...] = _rms(x, g_ref[...]).astype(BF16)
        o_ref[...] = x

        @pl.when(i == 0)
        def _():
            tail_ref[...] = jnp.zeros_like(tail_ref)

        for k in range(n_up):
            up_piece(k)

    @pl.when((f > 0) & (f < nf))
    def _():
        acts = []
        for k in range(n_up):
            up_piece(k)
            acts += [gate_piece(gpu * k + g) for g in range(gpu)]
        act = jnp.concatenate(acts, axis=0)
        o_ref[...] += jnp.dot(act, wd_ref[...], preferred_element_type=F32)

    @pl.when(f == nf)
    def _():
        act = jnp.concatenate([gate_piece(k) for k in range(n_gate)], axis=0)
        o_ref[...] += jnp.dot(act, wd_ref[...], preferred_element_type=F32)


def _conv_ffn(x2, g_ffn, w_up, conv_w, conv_b, w_down, s_len):
    t = x2.shape[0]
    tm, fc = FFN_TM, FFN_FC
    nf = D_FF // fc

    def up_idx(f):
        return jnp.minimum(f, nf - 1)

    def down_idx(f):
        return jnp.maximum(f - 1, 0)

    return pl.pallas_call(
        functools.partial(_ffn_kernel, s_len // tm),
        grid=(t // tm, nf + 1),
        in_specs=[
            pl.BlockSpec((tm, D_MODEL), lambda i, f: (i, 0), pipeline_mode=pl.Buffered(1)),
            pl.BlockSpec((1, D_MODEL), lambda i, f: (0, 0)),
            pl.BlockSpec((D_MODEL, fc), lambda i, f: (0, up_idx(f))),
            pl.BlockSpec((D_MODEL, fc), lambda i, f: (0, nf + up_idx(f))),
            pl.BlockSpec((CONV_W, fc), lambda i, f: (0, down_idx(f))),
            pl.BlockSpec((CONV_W, fc), lambda i, f: (0, nf + down_idx(f))),
            pl.BlockSpec((1, fc), lambda i, f: (0, down_idx(f))),
            pl.BlockSpec((1, fc), lambda i, f: (0, nf + down_idx(f))),
            pl.BlockSpec((fc, D_MODEL), lambda i, f: (down_idx(f), 0)),
        ],
        out_specs=pl.BlockSpec((tm, D_MODEL), lambda i, f: (i, 0)),
        out_shape=jax.ShapeDtypeStruct((t, D_MODEL), F32),
        scratch_shapes=[
            pltpu.VMEM((tm, D_MODEL), BF16),
            pltpu.VMEM((nf, SUBLANES, 2 * fc), F32),
            pltpu.VMEM((2, tm + SUBLANES, 2 * fc), F32),
        ],
        compiler_params=pltpu.CompilerParams(
            dimension_semantics=("arbitrary", "arbitrary"),
            vmem_limit_bytes=VMEM_LIMIT_BYTES),
        name="conv_ffn",
    )(x2, g_ffn, w_up, w_up, conv_w, conv_w, conv_b, conv_b, w_down)


def _reorder_in_cols(w_in):
    o = np.cumsum([0, A_WIDTH, A_WIDTH, B_WIDTH, B_KV_WIDTH, B_KV_WIDTH, C_WIDTH]).tolist()
    u_a, v_a, q_b, k_b, v_b, q_c = (w_in[:, o[n]:o[n + 1]] for n in range(6))
    return jnp.concatenate([w_in[:, MAIN_COLS:], q_b, u_a, v_a, q_c, k_b, v_b], axis=1)


def kernel(x, mem, positions, g_mix, w_in, g_a_v, w_spatial, b_spatial, g_b_q, g_b_k, sinks,
           g_mem, w_mem_kv, g_c_q, g_c_k, w_branch_a, w_branch_b, w_branch_c, w_out, g_ffn,
           w_up, conv_w, conv_b, w_down):
    bn, s_len, _ = x.shape
    depth = w_in.shape[0]
    assert s_len % MIX_TM == 0 and s_len % FFN_TM == 0 and (bn * s_len) % IN_TM == 0
    t = bn * s_len
    x2 = x.reshape(t, D_MODEL)
    pos2 = positions.astype(F32).reshape(t, 1)

    inv = ROPE_THETA ** (-jnp.arange(ROPE_HALF, dtype=F32) / ROPE_HALF)
    inv_head = jnp.concatenate([inv, inv, jnp.zeros((B_HEAD_DIM - ROPE_DIM,), F32)])
    invf = jnp.tile(inv_head, LANES // B_HEAD_DIM).reshape(1, LANES)
    grp = np.arange(MXU_DIM) // B_HEAD_DIM
    bd = jnp.asarray(grp[:, None] == grp[None, :], dtype=BF16)

    for l in range(depth):
        kct, vc = _mem_kv(mem, g_mem[l].reshape(1, D_MODEL), w_mem_kv[l].astype(BF16),
                          g_c_k[l].reshape(1, C_HEAD_DIM))
        proj, wb16 = _in_proj(
            x2, g_mix[l].reshape(1, D_MODEL), _reorder_in_cols(w_in[l]).astype(BF16),
            dict(w_up=w_up[l], w_down=w_down[l], w_out=w_out[l], w_branch_a=w_branch_a[l],
                 w_branch_b=w_branch_b[l], w_branch_c=w_branch_c[l]))
        bias_sp = jnp.repeat(b_spatial[l].T, A_GROUP_CH, axis=1)
        x2 = _mixers(
            x2, proj, pos2, kct, vc, sinks[l], g_a_v[l].reshape(1, A_WIDTH), w_spatial[l],
            bias_sp, jnp.tile(g_b_q[l], B_HEADS).reshape(1, B_WIDTH),
            jnp.tile(g_b_k[l], B_KV_HEADS).reshape(1, B_KV_WIDTH), invf,
            g_c_q[l].reshape(1, C_HEAD_DIM), bd,
            wb16["w_branch_a"], wb16["w_branch_b"], wb16["w_branch_c"], wb16["w_out"],
            bn, s_len)
        x2 = _conv_ffn(x2, g_ffn[l].reshape(1, D_MODEL), wb16["w_up"], conv_w[l],
                       conv_b[l].reshape(1, 2 * D_FF), wb16["w_down"], s_len)
    return x2.reshape(bn, s_len, D_MODEL)
```

```python
import functools

import jax
import jax.numpy as jnp
import numpy as np
from jax import lax
from jax.experimental import pallas as pl
from jax.experimental.pallas import tpu as pltpu

F32 = jnp.float32
BF16 = jnp.bfloat16

D_MODEL = 2048
MEM_LEN = 256
EPS = 1e-6
CHUNK = 128
A_GROUPS = 4
A_GROUP_CH = 128
A_WIDTH = A_GROUPS * A_GROUP_CH
WINDOW = 128
B_HEADS = 16
B_KV_HEADS = 2
B_REP = B_HEADS // B_KV_HEADS
B_HEAD_DIM = 64
B_WIDTH = B_HEADS * B_HEAD_DIM
B_KV_WIDTH = B_KV_HEADS * B_HEAD_DIM
ROPE_DIM = B_HEAD_DIM // 4
ROPE_HALF = ROPE_DIM // 2
ROPE_THETA = 500000.0
C_HEADS = 4
C_HEAD_DIM = 128
C_WIDTH = C_HEADS * C_HEAD_DIM
N_BRANCH = 3
D_FF = 5632
CONV_W = 3
MAIN_COLS = 2 * A_WIDTH + B_WIDTH + 2 * B_KV_WIDTH + C_WIDTH
IN_COLS = MAIN_COLS + N_BRANCH * D_MODEL

LANES = 128
SUBLANES = 8
MXU_DIM = 256
VMEM_LIMIT_BYTES = 56 * 1024 * 1024

IN_TM = 1024
IN_TN = 1280
MIX_TM = 256
FFN_TM = 1024
FFN_FC = 512

COL_QB = N_BRANCH * D_MODEL
COL_UV = COL_QB + B_WIDTH
COL_QC = COL_UV + 2 * A_WIDTH
COL_KV = COL_QC + C_WIDTH


def _sigmoid(x):
    return 0.5 * jnp.tanh(0.5 * x) + 0.5


def _rms(x, g):
    ms = jnp.mean(x * x, axis=-1, keepdims=True)
    return x * lax.rsqrt(ms + EPS) * g


def _mem_kv_kernel(mem_ref, g_ref, w_ref, gk_ref, kt_ref, v_ref):
    mh = _rms(mem_ref[0], g_ref[...]).astype(BF16)
    kv = jnp.dot(mh, w_ref[...], preferred_element_type=F32)
    for h in range(C_HEADS):
        k = _rms(kv[:, h * C_HEAD_DIM:(h + 1) * C_HEAD_DIM], gk_ref[...])
        kt_ref[0, h * C_HEAD_DIM:(h + 1) * C_HEAD_DIM, :] = k.T.astype(BF16)
    v_ref[0] = kv[:, C_WIDTH:].astype(BF16)


def _mem_kv(mem, g_mem, w_kv, g_c_k):
    bn = mem.shape[0]
    return pl.pallas_call(
        _mem_kv_kernel,
        grid=(bn,),
        in_specs=[
            pl.BlockSpec((1, MEM_LEN, D_MODEL), lambda b: (b, 0, 0)),
            pl.BlockSpec((1, D_MODEL), lambda b: (0, 0)),
            pl.BlockSpec((D_MODEL, 2 * C_WIDTH), lambda b: (0, 0)),
            pl.BlockSpec((1, C_HEAD_DIM), lambda b: (0, 0)),
        ],
        out_specs=[
            pl.BlockSpec((1, C_WIDTH, MEM_LEN), lambda b: (b, 0, 0)),
            pl.BlockSpec((1, MEM_LEN, C_WIDTH), lambda b: (b, 0, 0)),
        ],
        out_shape=[
            jax.ShapeDtypeStruct((bn, C_WIDTH, MEM_LEN), BF16),
            jax.ShapeDtypeStruct((bn, MEM_LEN, C_WIDTH), BF16),
        ],
        compiler_params=pltpu.CompilerParams(
            dimension_semantics=("arbitrary",), vmem_limit_bytes=VMEM_LIMIT_BYTES),
        name="mem_kv",
    )(mem, g_mem, w_kv, g_c_k)


def _cast_plan(n_steps):
    plan = [
        ("w_up", D_MODEL, 2 * D_FF, 32, 0),
        ("w_down", D_FF, D_MODEL, 128, 64),
        ("w_out", D_MODEL, D_MODEL, 64, 0),
        ("w_branch_b", B_WIDTH, D_MODEL, 64, 32),
        ("w_branch_a", A_WIDTH, D_MODEL, 64, 48),
        ("w_branch_c", C_WIDTH, D_MODEL, 64, 56),
    ]
    for _, rows, _, br, start in plan:
        assert rows % br == 0 and start + rows // br <= n_steps
    return plan


def _in_proj_kernel(plan, x_ref, g_ref, w_ref, *refs):
    n_side = len(plan)
    side_in, o_ref, side_out, h_ref = (refs[:n_side], refs[n_side],
                                       refs[n_side + 1:2 * n_side + 1], refs[-1])
    j = pl.program_id(1)
    step = pl.program_id(0) * pl.num_programs(1) + j

    for (_, rows, _, br, start), src, dst in zip(plan, side_in, side_out):
        @pl.when((step >= start) & (step < start + rows // br))
        def _():
            dst[...] = src[...].astype(BF16)

    @pl.when(j == 0)
    def _():
        h_ref[...] = _rms(x_ref[...], g_ref[...]).astype(BF16)

    def tile():
        return jnp.dot(h_ref[...], w_ref[...], preferred_element_type=F32)

    gate_tiles, gate_rem = divmod(COL_QB, IN_TN)
    uv_tile, uv_off = divmod(COL_UV, IN_TN)
    uv_end = COL_QC - (uv_tile + 1) * IN_TN
    assert gate_rem > 0 and uv_tile == gate_tiles + 1 and 0 < uv_end < IN_TN
    assert uv_tile + 2 == IN_COLS // IN_TN

    @pl.when(j < gate_tiles)
    def _():
        o_ref[...] = _sigmoid(tile()).astype(o_ref.dtype)

    @pl.when(j == gate_tiles)
    def _():
        acc = tile()
        o_ref[:, :gate_rem] = _sigmoid(acc[:, :gate_rem]).astype(o_ref.dtype)
        o_ref[:, gate_rem:] = acc[:, gate_rem:].astype(o_ref.dtype)

    @pl.when(j == uv_tile)
    def _():
        acc = tile()
        o_ref[:, :uv_off] = acc[:, :uv_off].astype(o_ref.dtype)
        o_ref[:, uv_off:] = jax.nn.gelu(acc[:, uv_off:]).astype(o_ref.dtype)

    @pl.when(j == uv_tile + 1)
    def _():
        acc = tile()
        o_ref[:, :uv_end] = jax.nn.gelu(acc[:, :uv_end]).astype(o_ref.dtype)
        o_ref[:, uv_end:] = acc[:, uv_end:].astype(o_ref.dtype)


def _in_proj(x2, g_mix, w_in_r, later_weights):
    t = x2.shape[0]
    ni, nj = t // IN_TM, IN_COLS // IN_TN
    plan = _cast_plan(ni * nj)

    def side_spec(rows, cols, br, start):
        nblk = rows // br
        return pl.BlockSpec((br, cols), lambda i, j: (jnp.clip(i * nj + j - start, 0, nblk - 1), 0))

    side_specs = [side_spec(rows, cols, br, start) for _, rows, cols, br, start in plan]
    outs = pl.pallas_call(
        functools.partial(_in_proj_kernel, plan),
        grid=(ni, nj),
        in_specs=[
            pl.BlockSpec((IN_TM, D_MODEL), lambda i, j: (i, 0)),
            pl.BlockSpec((1, D_MODEL), lambda i, j: (0, 0)),
            pl.BlockSpec((D_MODEL, IN_TN), lambda i, j: (0, j)),
        ] + side_specs,
        out_specs=[pl.BlockSpec((IN_TM, IN_TN), lambda i, j: (i, j))] + side_specs,
        out_shape=[jax.ShapeDtypeStruct((t, IN_COLS), BF16)]
        + [jax.ShapeDtypeStruct((rows, cols), BF16) for _, rows, cols, _, _ in plan],
        scratch_shapes=[pltpu.VMEM((IN_TM, D_MODEL), BF16)],
        compiler_params=pltpu.CompilerParams(
            dimension_semantics=("arbitrary", "arbitrary"),
            vmem_limit_bytes=VMEM_LIMIT_BYTES),
        name="in_proj",
    )(x2, g_mix, w_in_r, *[later_weights[name] for name, *_ in plan])
    return outs[0], {name: o for (name, *_), o in zip(plan, outs[1:])}


def _rope_tables(pos, invf):
    ang = pos * invf
    cos = jnp.cos(ang)
    sin = jnp.sin(ang)
    lane = lax.broadcasted_iota(jnp.int32, ang.shape, 1) % B_HEAD_DIM
    s_lo = jnp.where(lane < ROPE_HALF, -sin, 0.0)
    s_hi = jnp.where(lane >= ROPE_HALF, sin, 0.0)
    return cos, s_lo, s_hi


def _rope(z, tabs):
    cos, s_lo, s_hi = tabs
    return (z * cos + pltpu.roll(z, LANES - ROPE_HALF, 1) * s_lo
            + pltpu.roll(z, ROPE_HALF, 1) * s_hi)


def _head_norm(z, bd, g):
    ssq = jnp.dot((z * z).astype(BF16), bd, preferred_element_type=F32)
    return z * lax.rsqrt(ssq * (1.0 / B_HEAD_DIM) + EPS) * g


def _mixer_kernel(tps, sinks_ref, x_ref, g0_ref, g1_ref, g2_ref, qb_ref, uv_ref, qc_ref,
                  kv_ref, pos_ref, kct_ref, vc_ref,
                  gav_ref, wsp_ref, bsp_ref, gq_ref, gk_ref, invf_ref, gcq_ref, bd_ref,
                  wa_ref, wb_ref, wc_ref, wo_ref, o_ref, y0_ref, y1_ref, mg_ref,
                  kprev_ref, vprev_ref):
    step = pl.program_id(0)
    tile = jnp.minimum(step, pl.num_programs(0) - 2) % tps
    tm = x_ref.shape[0]
    n_chunks = tm // CHUNK

    @pl.when(step == 0)
    def _():
        y1_ref[...] = jnp.zeros_like(y1_ref)

    @pl.when(tile == 0)
    def _():
        kprev_ref[...] = jnp.zeros_like(kprev_ref)
        vprev_ref[...] = jnp.zeros_like(vprev_ref)

    def body(y_ref, yp_ref):
        att, tail = [], []
        st = {}
        bd = bd_ref[...]
        bd_kv = bd[:B_KV_WIDTH, :B_KV_WIDTH]
        row = lax.broadcasted_iota(jnp.int32, (CHUNK, CHUNK), 0)
        col = lax.broadcasted_iota(jnp.int32, (CHUNK, CHUNK), 1)

        def a_chunk(c):
            rows = slice(c * CHUNK, (c + 1) * CHUNK)
            u = uv_ref[rows, :A_WIDTH].astype(F32)
            v = _rms(uv_ref[rows, A_WIDTH:].astype(F32), gav_ref[...]).astype(BF16)
            s_ = jnp.concatenate(
                [jnp.dot(jnp.where(row >= col, wsp_ref[g], 0.0).astype(BF16),
                         v[:, g * A_GROUP_CH:(g + 1) * A_GROUP_CH],
                         preferred_element_type=F32) for g in range(A_GROUPS)], axis=1)
            y_ref[rows, 0:A_WIDTH] = (u * (s_ + bsp_ref[...])).astype(BF16)
        att += [functools.partial(a_chunk, c) for c in range(n_chunks)]

        def b_keys():
            tabs = _rope_tables(pos_ref[...], invf_ref[...])
            k_cur = _rope(_head_norm(kv_ref[:, :B_KV_WIDTH].astype(F32), bd_kv, gk_ref[...]), tabs)
            st["tabs"] = tabs
            st["kt_all"] = jnp.concatenate([kprev_ref[...], k_cur], axis=0).T.astype(BF16)
            st["v_all"] = jnp.concatenate([vprev_ref[...], kv_ref[:, B_KV_WIDTH:]], axis=0)
            kprev_ref[...] = k_cur[tm - WINDOW:]
            vprev_ref[...] = kv_ref[tm - WINDOW:, B_KV_WIDTH:]
            st["q"] = []
        att.append(b_keys)

        scale = B_HEAD_DIM ** -0.5

        def b_q(j):
            z = _head_norm(qb_ref[:, j * MXU_DIM:(j + 1) * MXU_DIM].astype(F32), bd,
                           gq_ref[:, j * MXU_DIM:(j + 1) * MXU_DIM])
            for half in range(MXU_DIM // LANES):
                zz = _rope(z[:, half * LANES:(half + 1) * LANES], st["tabs"]) * scale
                st["q"].append(zz.astype(BF16))
        att += [functools.partial(b_q, j) for j in range(B_WIDTH // MXU_DIM)]

        upper = col > row
        lane_lo = col < B_HEAD_DIM
        spk = B_REP // 2

        def b_attend(c, h):
            zk = jnp.zeros((B_HEAD_DIM, 2 * WINDOW), BF16)
            zv = jnp.zeros((2 * WINDOW, B_HEAD_DIM), BF16)
            ov = jnp.ones((2 * WINDOW, B_HEAD_DIM), BF16)
            rows = slice(c * CHUNK, (c + 1) * CHUNK)
            kt = st["kt_all"][h * B_HEAD_DIM:(h + 1) * B_HEAD_DIM, c * CHUNK:c * CHUNK + 2 * WINDOW]
            v2 = st["v_all"][c * CHUNK:c * CHUNK + 2 * WINDOW, h * B_HEAD_DIM:(h + 1) * B_HEAD_DIM]
            k_bd = jnp.concatenate([jnp.concatenate([kt, zk], axis=1),
                                    jnp.concatenate([zk, kt], axis=1)], axis=0)
            v_bd = jnp.concatenate([jnp.concatenate([v2, zv, ov, zv], axis=1),
                                    jnp.concatenate([zv, v2, zv, ov], axis=1)], axis=0)
            q_stack = jnp.concatenate([st["q"][h * spk + j][rows] for j in range(spk)], axis=0)
            s_all = jnp.dot(q_stack, k_bd, preferred_element_type=F32)
            p_rows, e_rows = [], []
            for j in range(spk):
                p_pair, e_pair = [], []
                for half in range(2):
                    blk = s_all[j * CHUNK:(j + 1) * CHUNK,
                                half * 2 * WINDOW:(half + 1) * 2 * WINDOW]
                    s_prev = blk[:, :WINDOW]
                    if c == 0:
                        s_prev = jnp.where(tile > 0, s_prev, -jnp.inf)
                    s_ = jnp.where(upper, s_prev, blk[:, WINDOW:])
                    sink = sinks_ref[h * B_REP + 2 * j + half]
                    m = jnp.maximum(jnp.max(s_, axis=-1, keepdims=True), sink)
                    p = jnp.exp(s_ - m)
                    e_pair.append(jnp.exp(sink - m))
                    p_pair += [jnp.where(upper, p, 0.0), jnp.where(upper, 0.0, p)]
                p_rows.append(jnp.concatenate(p_pair, axis=1).astype(BF16))
                e_rows.append(jnp.where(lane_lo, e_pair[0], e_pair[1]))
            o_all = jnp.dot(jnp.concatenate(p_rows, axis=0), v_bd,
                            preferred_element_type=F32)
            for j in range(spk):
                o = o_all[j * CHUNK:(j + 1) * CHUNK]
                c0 = A_WIDTH + (h * spk + j) * LANES
                y_ref[rows, c0:c0 + LANES] = (o[:, :LANES] / (o[:, LANES:] + e_rows[j])).astype(BF16)
        att += [functools.partial(b_attend, c, h) for c in range(n_chunks) for h in range(B_KV_HEADS)]

        c_scale = C_HEAD_DIM ** -0.5

        def c_head(h):
            cols = slice(h * C_HEAD_DIM, (h + 1) * C_HEAD_DIM)
            qn = _rms(qc_ref[:, cols].astype(F32), gcq_ref[...]).astype(BF16)
            s_ = jnp.dot(qn, kct_ref[0, cols, :], preferred_element_type=F32) * c_scale
            m = jnp.max(s_, axis=-1, keepdims=True)
            p = jnp.exp(s_ - m)
            l = jnp.sum(p, axis=-1, keepdims=True)
            o = jnp.dot(p.astype(BF16), vc_ref[0, :, cols], preferred_element_type=F32) / l
            c0 = A_WIDTH + B_WIDTH + h * C_HEAD_DIM
            y_ref[:, c0:c0 + C_HEAD_DIM] = o.astype(BF16)
        att += [functools.partial(c_head, h) for h in range(C_HEADS)]

        def merge_cols(cb):
            cols = slice(cb * MXU_DIM, (cb + 1) * MXU_DIM)
            za = jnp.dot(yp_ref[:, 0:A_WIDTH], wa_ref[:, cols], preferred_element_type=F32)
            zb = jnp.dot(yp_ref[:, A_WIDTH:A_WIDTH + B_WIDTH], wb_ref[:, cols],
                         preferred_element_type=F32)
            zc = jnp.dot(yp_ref[:, A_WIDTH + B_WIDTH:], wc_ref[:, cols],
                         preferred_element_type=F32)
            mg_ref[:, cols] = (g0_ref[:, cols].astype(F32) * za + g1_ref[:, cols].astype(F32) * zb
                               + g2_ref[:, cols].astype(F32) * zc).astype(BF16)

        def out_cols(cb):
            cols = slice(cb * MXU_DIM, (cb + 1) * MXU_DIM)
            o_ref[:, cols] = x_ref[:, cols] + jnp.dot(mg_ref[...], wo_ref[:, cols],
                                                      preferred_element_type=F32)
        n_cb = D_MODEL // MXU_DIM
        tail += [functools.partial(merge_cols, cb) for cb in range(n_cb)]
        tail += [functools.partial(out_cols, cb) for cb in range(n_cb)]

        n_a, n_t = len(att), len(tail)
        ti = 0
        for ai, piece in enumerate(att):
            piece()
            while ti < n_t and (ti + 1) * n_a <= (ai + 1) * n_t:
                tail[ti]()
                ti += 1
        while ti < n_t:
            tail[ti]()
            ti += 1

    for par, (y_ref, yp_ref) in enumerate(((y0_ref, y1_ref), (y1_ref, y0_ref))):
        @pl.when(step % 2 == par)
        def _():
            body(y_ref, yp_ref)


def _mixers(x2, proj, pos2, kct, vc, sinks, g_a_v, w_sp, bias_sp, gq_t, gk_t, invf, gcq,
            bd, wa, wb, wc, wo, bn, s_len):
    t = x2.shape[0]
    tm = MIX_TM
    tps = s_len // tm
    n_tiles = t // tm

    def cur(s):
        return jnp.minimum(s, n_tiles - 1)

    def prev(s):
        return jnp.maximum(s - 1, 0)

    def const(*idx):
        return lambda s: idx

    in_specs = [
        pl.BlockSpec(memory_space=pltpu.SMEM),
        pl.BlockSpec((tm, D_MODEL), lambda s: (prev(s), 0)),
        pl.BlockSpec((tm, D_MODEL), lambda s: (prev(s), 0)),
        pl.BlockSpec((tm, D_MODEL), lambda s: (prev(s), 1)),
        pl.BlockSpec((tm, D_MODEL), lambda s: (prev(s), 2)),
        pl.BlockSpec((tm, B_WIDTH), lambda s: (cur(s), COL_QB // B_WIDTH)),
        pl.BlockSpec((tm, 2 * A_WIDTH), lambda s: (cur(s), COL_UV // (2 * A_WIDTH))),
        pl.BlockSpec((tm, C_WIDTH), lambda s: (cur(s), COL_QC // C_WIDTH)),
        pl.BlockSpec((tm, 2 * B_KV_WIDTH), lambda s: (cur(s), COL_KV // (2 * B_KV_WIDTH))),
        pl.BlockSpec((tm, 1), lambda s: (cur(s), 0)),
        pl.BlockSpec((1, C_WIDTH, MEM_LEN), lambda s: (cur(s) // tps, 0, 0)),
        pl.BlockSpec((1, MEM_LEN, C_WIDTH), lambda s: (cur(s) // tps, 0, 0)),
        pl.BlockSpec((1, A_WIDTH), const(0, 0)),
        pl.BlockSpec((A_GROUPS, CHUNK, CHUNK), const(0, 0, 0)),
        pl.BlockSpec((CHUNK, A_WIDTH), const(0, 0)),
        pl.BlockSpec((1, B_WIDTH), const(0, 0)),
        pl.BlockSpec((1, B_KV_WIDTH), const(0, 0)),
        pl.BlockSpec((1, LANES), const(0, 0)),
        pl.BlockSpec((1, C_HEAD_DIM), const(0, 0)),
        pl.BlockSpec((MXU_DIM, MXU_DIM), const(0, 0)),
        pl.BlockSpec((A_WIDTH, D_MODEL), const(0, 0), pipeline_mode=pl.Buffered(1)),
        pl.BlockSpec((B_WIDTH, D_MODEL), const(0, 0), pipeline_mode=pl.Buffered(1)),
        pl.BlockSpec((C_WIDTH, D_MODEL), const(0, 0), pipeline_mode=pl.Buffered(1)),
        pl.BlockSpec((D_MODEL, D_MODEL), const(0, 0), pipeline_mode=pl.Buffered(1)),
    ]
    return pl.pallas_call(
        functools.partial(_mixer_kernel, tps),
        grid=(n_tiles + 1,),
        in_specs=in_specs,
        out_specs=pl.BlockSpec((tm, D_MODEL), lambda s: (prev(s), 0)),
        out_shape=jax.ShapeDtypeStruct((t, D_MODEL), F32),
        scratch_shapes=[
            pltpu.VMEM((tm, D_MODEL), BF16),
            pltpu.VMEM((tm, D_MODEL), BF16),
            pltpu.VMEM((tm, D_MODEL), BF16),
            pltpu.VMEM((WINDOW, B_KV_WIDTH), F32),
            pltpu.VMEM((WINDOW, B_KV_WIDTH), BF16),
        ],
        compiler_params=pltpu.CompilerParams(
            dimension_semantics=("arbitrary",),
            vmem_limit_bytes=VMEM_LIMIT_BYTES),
        name="mixers",
    )(sinks, x2, proj, proj, proj, proj, proj, proj, proj, pos2, kct, vc,
      g_a_v, w_sp, bias_sp, gq_t, gk_t, invf, gcq, bd, wa, wb, wc, wo)


def _ffn_kernel(tiles_per_seq, x_hbm, g_ref, wua_ref, wub_ref, cwa_ref, cwb_ref, cba_ref,
                cbb_ref, wd_ref, o_ref, h_ref, tail_ref, ubuf_ref, xbuf_ref, x_sem):
    i = pl.program_id(0)
    f = pl.program_id(1)
    nf = pl.num_programs(1) - 1
    tm = xbuf_ref.shape[0]
    fc = wd_ref.shape[0]
    ppw = fc // MXU_DIM
    n_up = 2 * ppw

    def x_copy(tile_idx):
        return pltpu.make_async_copy(x_hbm.at[pl.ds(tile_idx * tm, tm), :], xbuf_ref, x_sem)
    n_gate = tm // CHUNK
    gpu = n_gate // n_up

    def up_piece(k):
        slot = f % 2
        w_ref = wua_ref if k < ppw else wub_ref
        wcols = slice((k % ppw) * MXU_DIM, (k % ppw + 1) * MXU_DIM)
        cols = slice(k * MXU_DIM, (k + 1) * MXU_DIM)
        up = jnp.dot(h_ref[...], w_ref[:, wcols], preferred_element_type=F32)
        seq_start = i % tiles_per_seq == 0
        ubuf_ref[slot, 0:SUBLANES, cols] = jnp.where(seq_start, 0.0, tail_ref[f, :, cols])
        ubuf_ref[slot, SUBLANES:, cols] = up
        tail_ref[f, :, cols] = up[tm - SUBLANES:]

    def gate_piece(k):
        slot = (f + 1) % 2
        r0 = SUBLANES + k * CHUNK
        cw = jnp.concatenate([cwa_ref[...], cwb_ref[...]], axis=1)
        cb = jnp.concatenate([cba_ref[...], cbb_ref[...]], axis=1)
        conv = (cw[0:1] * ubuf_ref[slot, r0 - 2:r0 - 2 + CHUNK, :]
                + cw[1:2] * ubuf_ref[slot, r0 - 1:r0 - 1 + CHUNK, :]
                + cw[2:3] * ubuf_ref[slot, r0:r0 + CHUNK, :] + cb)
        ha = 0.5 * conv[:, :fc]
        return ((ha + ha * jnp.tanh(ha)) * conv[:, fc:]).astype(BF16)

    @pl.when(f == 0)
    def _():
        @pl.when(i == 0)
        def _():
            x_copy(i).start()
            tail_ref[...] = jnp.zeros_like(tail_ref)

        x_copy(i).wait()
        x = xbuf_ref[...]
        h_ref[...] = _rms(x, g_ref[...]).astype(BF16)
        o_ref[...] = x
        for k in range(n_up):
            up_piece(k)

    @pl.when((f == 1) & (i + 1 < pl.num_programs(0)))
    def _():
        x_copy(i + 1).start()

    @pl.when((f > 0) & (f < nf))
    def _():
        acts = []
        for k in range(n_up):
            up_piece(k)
            acts += [gate_piece(gpu * k + g) for g in range(gpu)]
        act = jnp.concatenate(acts, axis=0)
        o_ref[...] += jnp.dot(act, wd_ref[...], preferred_element_type=F32)

    @pl.when(f == nf)
    def _():
        act = jnp.concatenate([gate_piece(k) for k in range(n_gate)], axis=0)
        o_ref[...] += jnp.dot(act, wd_ref[...], preferred_element_type=F32)


def _conv_ffn(x2, g_ffn, w_up, conv_w, conv_b, w_down, s_len):
    t = x2.shape[0]
    tm, fc = FFN_TM, FFN_FC
    nf = D_FF // fc

    def up_idx(f):
        return jnp.minimum(f, nf - 1)

    def down_idx(f):
        return jnp.maximum(f - 1, 0)

    return pl.pallas_call(
        functools.partial(_ffn_kernel, s_len // tm),
        grid=(t // tm, nf + 1),
        in_specs=[
            pl.BlockSpec(memory_space=pl.ANY),
            pl.BlockSpec((1, D_MODEL), lambda i, f: (0, 0)),
            pl.BlockSpec((D_MODEL, fc), lambda i, f: (0, up_idx(f))),
            pl.BlockSpec((D_MODEL, fc), lambda i, f: (0, nf + up_idx(f))),
            pl.BlockSpec((CONV_W, fc), lambda i, f: (0, down_idx(f))),
            pl.BlockSpec((CONV_W, fc), lambda i, f: (0, nf + down_idx(f))),
            pl.BlockSpec((1, fc), lambda i, f: (0, down_idx(f))),
            pl.BlockSpec((1, fc), lambda i, f: (0, nf + down_idx(f))),
            pl.BlockSpec((fc, D_MODEL), lambda i, f: (down_idx(f), 0)),
        ],
        out_specs=pl.BlockSpec((tm, D_MODEL), lambda i, f: (i, 0)),
        out_shape=jax.ShapeDtypeStruct((t, D_MODEL), F32),
        scratch_shapes=[
            pltpu.VMEM((tm, D_MODEL), BF16),
            pltpu.VMEM((nf, SUBLANES, 2 * fc), F32),
            pltpu.VMEM((2, tm + SUBLANES, 2 * fc), F32),
            pltpu.VMEM((tm, D_MODEL), F32),
            pltpu.SemaphoreType.DMA(()),
        ],
        compiler_params=pltpu.CompilerParams(
            dimension_semantics=("arbitrary", "arbitrary"),
            vmem_limit_bytes=VMEM_LIMIT_BYTES),
        name="conv_ffn",
    )(x2, g_ffn, w_up, w_up, conv_w, conv_w, conv_b, conv_b, w_down)


def _in_col_pieces():
    o = np.cumsum([0, A_WIDTH, A_WIDTH, B_WIDTH, B_KV_WIDTH, B_KV_WIDTH, C_WIDTH]).tolist()
    u_a, v_a, q_b, k_b, v_b, q_c = ((o[n], o[n + 1] - o[n]) for n in range(6))
    return [(MAIN_COLS, N_BRANCH * D_MODEL), q_b, u_a, v_a, q_c, k_b, v_b]


def _reorder_cast_kernel(w_ref, o_ref):
    dst = 0
    for src, width in _in_col_pieces():
        o_ref[:, dst:dst + width] = w_ref[:, src:src + width].astype(BF16)
        dst += width


def _reorder_in_cols(w_in):
    rows = 256
    return pl.pallas_call(
        _reorder_cast_kernel,
        grid=(D_MODEL // rows,),
        in_specs=[pl.BlockSpec((rows, IN_COLS), lambda r: (r, 0))],
        out_specs=pl.BlockSpec((rows, IN_COLS), lambda r: (r, 0)),
        out_shape=jax.ShapeDtypeStruct((D_MODEL, IN_COLS), BF16),
        compiler_params=pltpu.CompilerParams(
            dimension_semantics=("arbitrary",), vmem_limit_bytes=VMEM_LIMIT_BYTES),
        name="w_in_cast",
    )(w_in)


def kernel(x, mem, positions, g_mix, w_in, g_a_v, w_spatial, b_spatial, g_b_q, g_b_k, sinks,
           g_mem, w_mem_kv, g_c_q, g_c_k, w_branch_a, w_branch_b, w_branch_c, w_out, g_ffn,
           w_up, conv_w, conv_b, w_down):
    bn, s_len, _ = x.shape
    depth = w_in.shape[0]
    assert s_len % MIX_TM == 0 and s_len % FFN_TM == 0 and (bn * s_len) % IN_TM == 0
    t = bn * s_len
    x2 = x.reshape(t, D_MODEL)
    pos2 = positions.astype(F32).reshape(t, 1)

    inv = ROPE_THETA ** (-jnp.arange(ROPE_HALF, dtype=F32) / ROPE_HALF)
    inv_head = jnp.concatenate([inv, inv, jnp.zeros((B_HEAD_DIM - ROPE_DIM,), F32)])
    invf = jnp.tile(inv_head, LANES // B_HEAD_DIM).reshape(1, LANES)
    grp = np.arange(MXU_DIM) // B_HEAD_DIM
    bd = jnp.asarray(grp[:, None] == grp[None, :], dtype=BF16)

    for l in range(depth):
        kct, vc = _mem_kv(mem, g_mem[l].reshape(1, D_MODEL), w_mem_kv[l].astype(BF16),
                          g_c_k[l].reshape(1, C_HEAD_DIM))
        proj, wb16 = _in_proj(
            x2, g_mix[l].reshape(1, D_MODEL), _reorder_in_cols(w_in[l]),
            dict(w_up=w_up[l], w_down=w_down[l], w_out=w_out[l], w_branch_a=w_branch_a[l],
                 w_branch_b=w_branch_b[l], w_branch_c=w_branch_c[l]))
        bias_sp = jnp.repeat(b_spatial[l].T, A_GROUP_CH, axis=1)
        x2 = _mixers(
            x2, proj, pos2, kct, vc, sinks[l], g_a_v[l].reshape(1, A_WIDTH), w_spatial[l],
            bias_sp, jnp.tile(g_b_q[l], B_HEADS).reshape(1, B_WIDTH),
            jnp.tile(g_b_k[l], B_KV_HEADS).reshape(1, B_KV_WIDTH), invf,
            g_c_q[l].reshape(1, C_HEAD_DIM), bd,
            wb16["w_branch_a"], wb16["w_branch_b"], wb16["w_branch_c"], wb16["w_out"],
            bn, s_len)
        x2 = _conv_ffn(x2, g_ffn[l].reshape(1, D_MODEL), wb16["w_up"], conv_w[l],
                       conv_b[l].reshape(1, 2 * D_FF), wb16["w_down"], s_len)
    return x2.reshape(bn, s_len, D_MODEL)
```

```python
import functools

import jax
import jax.numpy as jnp
import numpy as np
from jax import lax
from jax.experimental import pallas as pl
from jax.experimental.pallas import tpu as pltpu

F32 = jnp.float32
BF16 = jnp.bfloat16

D_MODEL = 2048
MEM_LEN = 256
EPS = 1e-6
CHUNK = 128
A_GROUPS = 4
A_GROUP_CH = 128
A_WIDTH = A_GROUPS * A_GROUP_CH
WINDOW = 128
B_HEADS = 16
B_KV_HEADS = 2
B_REP = B_HEADS // B_KV_HEADS
B_HEAD_DIM = 64
B_WIDTH = B_HEADS * B_HEAD_DIM
B_KV_WIDTH = B_KV_HEADS * B_HEAD_DIM
ROPE_DIM = B_HEAD_DIM // 4
ROPE_HALF = ROPE_DIM // 2
ROPE_THETA = 500000.0
C_HEADS = 4
C_HEAD_DIM = 128
C_WIDTH = C_HEADS * C_HEAD_DIM
N_BRANCH = 3
D_FF = 5632
CONV_W = 3
MAIN_COLS = 2 * A_WIDTH + B_WIDTH + 2 * B_KV_WIDTH + C_WIDTH
IN_COLS = MAIN_COLS + N_BRANCH * D_MODEL

LANES = 128
SUBLANES = 8
MXU_DIM = 256
VMEM_LIMIT_BYTES = 56 * 1024 * 1024

IN_TM = 1024
IN_TN = 1280
MIX_TM = 256
FFN_TM = 1024
FFN_FC = 512

COL_QB = N_BRANCH * D_MODEL
COL_UV = COL_QB + B_WIDTH
COL_QC = COL_UV + 2 * A_WIDTH
COL_KV = COL_QC + C_WIDTH


def _sigmoid(x):
    return 0.5 * jnp.tanh(0.5 * x) + 0.5


def _rms(x, g):
    ms = jnp.mean(x * x, axis=-1, keepdims=True)
    return x * lax.rsqrt(ms + EPS) * g


def _mem_kv_kernel(mem_ref, g_ref, w_ref, gk_ref, kt_ref, v_ref):
    mh = _rms(mem_ref[0], g_ref[...]).astype(BF16)
    kv = jnp.dot(mh, w_ref[...], preferred_element_type=F32)
    for h in range(C_HEADS):
        k = _rms(kv[:, h * C_HEAD_DIM:(h + 1) * C_HEAD_DIM], gk_ref[...])
        kt_ref[0, h * C_HEAD_DIM:(h + 1) * C_HEAD_DIM, :] = k.T.astype(BF16)
    v_ref[0] = kv[:, C_WIDTH:].astype(BF16)


def _mem_kv(mem, g_mem, w_kv, g_c_k):
    bn = mem.shape[0]
    return pl.pallas_call(
        _mem_kv_kernel,
        grid=(bn,),
        in_specs=[
            pl.BlockSpec((1, MEM_LEN, D_MODEL), lambda b: (b, 0, 0)),
            pl.BlockSpec((1, D_MODEL), lambda b: (0, 0)),
            pl.BlockSpec((D_MODEL, 2 * C_WIDTH), lambda b: (0, 0)),
            pl.BlockSpec((1, C_HEAD_DIM), lambda b: (0, 0)),
        ],
        out_specs=[
            pl.BlockSpec((1, C_WIDTH, MEM_LEN), lambda b: (b, 0, 0)),
            pl.BlockSpec((1, MEM_LEN, C_WIDTH), lambda b: (b, 0, 0)),
        ],
        out_shape=[
            jax.ShapeDtypeStruct((bn, C_WIDTH, MEM_LEN), BF16),
            jax.ShapeDtypeStruct((bn, MEM_LEN, C_WIDTH), BF16),
        ],
        compiler_params=pltpu.CompilerParams(
            dimension_semantics=("arbitrary",), vmem_limit_bytes=VMEM_LIMIT_BYTES),
        name="mem_kv",
    )(mem, g_mem, w_kv, g_c_k)


def _cast_plan(n_steps):
    plan = [
        ("w_up", D_MODEL, 2 * D_FF, 32, 0),
        ("w_down", D_FF, D_MODEL, 128, 64),
        ("w_out", D_MODEL, D_MODEL, 64, 0),
        ("w_branch_b", B_WIDTH, D_MODEL, 64, 32),
        ("w_branch_a", A_WIDTH, D_MODEL, 64, 48),
        ("w_branch_c", C_WIDTH, D_MODEL, 64, 56),
    ]
    for _, rows, _, br, start in plan:
        assert rows % br == 0 and start + rows // br <= n_steps
    return plan


def _in_proj_kernel(plan, x_ref, g_ref, w_ref, *refs):
    n_side = len(plan)
    side_in, o_ref, side_out, h_ref = (refs[:n_side], refs[n_side],
                                       refs[n_side + 1:2 * n_side + 1], refs[-1])
    j = pl.program_id(1)
    step = pl.program_id(0) * pl.num_programs(1) + j

    for (_, rows, _, br, start), src, dst in zip(plan, side_in, side_out):
        @pl.when((step >= start) & (step < start + rows // br))
        def _():
            dst[...] = src[...].astype(BF16)

    def tile():
        return jnp.dot(h_ref[...], w_ref[...], preferred_element_type=F32)

    gate_tiles, gate_rem = divmod(COL_QB, IN_TN)
    uv_tile, uv_off = divmod(COL_UV, IN_TN)
    uv_end = COL_QC - (uv_tile + 1) * IN_TN
    assert gate_rem > 0 and uv_tile == gate_tiles + 1 and 0 < uv_end < IN_TN
    assert uv_tile + 2 == IN_COLS // IN_TN and gate_tiles > 1

    @pl.when(j == 0)
    def _():
        for r in range(0, IN_TM, MXU_DIM):
            rows = slice(r, r + MXU_DIM)
            h = _rms(x_ref[rows, :], g_ref[...]).astype(BF16)
            h_ref[rows, :] = h
            o_ref[rows, :] = _sigmoid(
                jnp.dot(h, w_ref[...], preferred_element_type=F32)).astype(o_ref.dtype)

    @pl.when((j > 0) & (j < gate_tiles))
    def _():
        o_ref[...] = _sigmoid(tile()).astype(o_ref.dtype)

    @pl.when(j == gate_tiles)
    def _():
        acc = tile()
        o_ref[:, :gate_rem] = _sigmoid(acc[:, :gate_rem]).astype(o_ref.dtype)
        o_ref[:, gate_rem:] = acc[:, gate_rem:].astype(o_ref.dtype)

    @pl.when(j == uv_tile)
    def _():
        acc = tile()
        o_ref[:, :uv_off] = acc[:, :uv_off].astype(o_ref.dtype)
        o_ref[:, uv_off:] = jax.nn.gelu(acc[:, uv_off:]).astype(o_ref.dtype)

    @pl.when(j == uv_tile + 1)
    def _():
        acc = tile()
        o_ref[:, :uv_end] = jax.nn.gelu(acc[:, :uv_end]).astype(o_ref.dtype)
        o_ref[:, uv_end:] = acc[:, uv_end:].astype(o_ref.dtype)


def _in_proj(x2, g_mix, w_in_r, later_weights):
    t = x2.shape[0]
    ni, nj = t // IN_TM, IN_COLS // IN_TN
    plan = _cast_plan(ni * nj)

    def side_spec(rows, cols, br, start):
        nblk = rows // br
        return pl.BlockSpec((br, cols), lambda i, j: (jnp.clip(i * nj + j - start, 0, nblk - 1), 0))

    side_specs = [side_spec(rows, cols, br, start) for _, rows, cols, br, start in plan]
    outs = pl.pallas_call(
        functools.partial(_in_proj_kernel, plan),
        grid=(ni, nj),
        in_specs=[
            pl.BlockSpec((IN_TM, D_MODEL), lambda i, j: (i, 0)),
            pl.BlockSpec((1, D_MODEL), lambda i, j: (0, 0)),
            pl.BlockSpec((D_MODEL, IN_TN), lambda i, j: (0, j)),
        ] + side_specs,
        out_specs=[pl.BlockSpec((IN_TM, IN_TN), lambda i, j: (i, j))] + side_specs,
        out_shape=[jax.ShapeDtypeStruct((t, IN_COLS), BF16)]
        + [jax.ShapeDtypeStruct((rows, cols), BF16) for _, rows, cols, _, _ in plan],
        scratch_shapes=[pltpu.VMEM((IN_TM, D_MODEL), BF16)],
        compiler_params=pltpu.CompilerParams(
            dimension_semantics=("arbitrary", "arbitrary"),
            vmem_limit_bytes=VMEM_LIMIT_BYTES),
        name="in_proj",
    )(x2, g_mix, w_in_r, *[later_weights[name] for name, *_ in plan])
    return outs[0], {name: o for (name, *_), o in zip(plan, outs[1:])}


def _rope_tables(pos, invf):
    ang = pos * invf
    cos = jnp.cos(ang)
    sin = jnp.sin(ang)
    lane = lax.broadcasted_iota(jnp.int32, ang.shape, 1) % B_HEAD_DIM
    s_lo = jnp.where(lane < ROPE_HALF, -sin, 0.0)
    s_hi = jnp.where(lane >= ROPE_HALF, sin, 0.0)
    return cos, s_lo, s_hi


def _rope(z, tabs):
    cos, s_lo, s_hi = tabs
    return (z * cos + pltpu.roll(z, LANES - ROPE_HALF, 1) * s_lo
            + pltpu.roll(z, ROPE_HALF, 1) * s_hi)


def _head_norm(z, bd, g):
    ssq = jnp.dot((z * z).astype(BF16), bd, preferred_element_type=F32)
    return z * lax.rsqrt(ssq * (1.0 / B_HEAD_DIM) + EPS) * g


def _mixer_kernel(tps, sinks_ref, x_ref, g0_ref, g1_ref, g2_ref, qb_ref, uv_ref, qc_ref,
                  kv_ref, pos_ref, kct_ref, vc_ref,
                  gav_ref, wsp_ref, bsp_ref, gq_ref, gk_ref, invf_ref, gcq_ref, bd_ref,
                  wa_ref, wb_ref, wc_ref, wo_ref, o_ref, y0_ref, y1_ref, mg_ref,
                  kprev_ref, vprev_ref):
    step = pl.program_id(0)
    tile = jnp.minimum(step, pl.num_programs(0) - 2) % tps
    tm = x_ref.shape[0]
    n_chunks = tm // CHUNK

    @pl.when(step == 0)
    def _():
        y1_ref[...] = jnp.zeros_like(y1_ref)

    @pl.when(tile == 0)
    def _():
        kprev_ref[...] = jnp.zeros_like(kprev_ref)
        vprev_ref[...] = jnp.zeros_like(vprev_ref)

    def body(y_ref, yp_ref):
        att, tail = [], []
        st = {}
        bd = bd_ref[...]
        bd_kv = bd[:B_KV_WIDTH, :B_KV_WIDTH]
        row = lax.broadcasted_iota(jnp.int32, (CHUNK, CHUNK), 0)
        col = lax.broadcasted_iota(jnp.int32, (CHUNK, CHUNK), 1)

        def a_chunk(c):
            rows = slice(c * CHUNK, (c + 1) * CHUNK)
            u = uv_ref[rows, :A_WIDTH].astype(F32)
            v = _rms(uv_ref[rows, A_WIDTH:].astype(F32), gav_ref[...]).astype(BF16)
            s_ = jnp.concatenate(
                [jnp.dot(jnp.where(row >= col, wsp_ref[g], 0.0).astype(BF16),
                         v[:, g * A_GROUP_CH:(g + 1) * A_GROUP_CH],
                         preferred_element_type=F32) for g in range(A_GROUPS)], axis=1)
            y_ref[rows, 0:A_WIDTH] = (u * (s_ + bsp_ref[...])).astype(BF16)
        att += [functools.partial(a_chunk, c) for c in range(n_chunks)]

        def b_keys():
            tabs = _rope_tables(pos_ref[...], invf_ref[...])
            k_cur = _rope(_head_norm(kv_ref[:, :B_KV_WIDTH].astype(F32), bd_kv, gk_ref[...]), tabs)
            st["tabs"] = tabs
            st["kt_all"] = jnp.concatenate([kprev_ref[...], k_cur], axis=0).T.astype(BF16)
            st["v_all"] = jnp.concatenate([vprev_ref[...], kv_ref[:, B_KV_WIDTH:]], axis=0)
            kprev_ref[...] = k_cur[tm - WINDOW:]
            vprev_ref[...] = kv_ref[tm - WINDOW:, B_KV_WIDTH:]
            st["q"] = []
        att.append(b_keys)

        scale = B_HEAD_DIM ** -0.5

        def b_q(j):
            z = _head_norm(qb_ref[:, j * MXU_DIM:(j + 1) * MXU_DIM].astype(F32), bd,
                           gq_ref[:, j * MXU_DIM:(j + 1) * MXU_DIM])
            for half in range(MXU_DIM // LANES):
                zz = _rope(z[:, half * LANES:(half + 1) * LANES], st["tabs"]) * scale
                st["q"].append(zz.astype(BF16))
        att += [functools.partial(b_q, j) for j in range(B_WIDTH // MXU_DIM)]

        upper = col > row
        lane_lo = col < B_HEAD_DIM
        spk = B_REP // 2

        def b_attend(c, h):
            zk = jnp.zeros((B_HEAD_DIM, 2 * WINDOW), BF16)
            zv = jnp.zeros((2 * WINDOW, B_HEAD_DIM), BF16)
            ov = jnp.ones((2 * WINDOW, B_HEAD_DIM), BF16)
            rows = slice(c * CHUNK, (c + 1) * CHUNK)
            kt = st["kt_all"][h * B_HEAD_DIM:(h + 1) * B_HEAD_DIM, c * CHUNK:c * CHUNK + 2 * WINDOW]
            v2 = st["v_all"][c * CHUNK:c * CHUNK + 2 * WINDOW, h * B_HEAD_DIM:(h + 1) * B_HEAD_DIM]
            k_bd = jnp.concatenate([jnp.concatenate([kt, zk], axis=1),
                                    jnp.concatenate([zk, kt], axis=1)], axis=0)
            v_bd = jnp.concatenate([jnp.concatenate([v2, zv, ov, zv], axis=1),
                                    jnp.concatenate([zv, v2, zv, ov], axis=1)], axis=0)
            q_stack = jnp.concatenate([st["q"][h * spk + j][rows] for j in range(spk)], axis=0)
            s_all = jnp.dot(q_stack, k_bd, preferred_element_type=F32)
            p_rows, e_rows = [], []
            for j in range(spk):
                p_pair, e_pair = [], []
                for half in range(2):
                    blk = s_all[j * CHUNK:(j + 1) * CHUNK,
                                half * 2 * WINDOW:(half + 1) * 2 * WINDOW]
                    s_prev = blk[:, :WINDOW]
                    if c == 0:
                        s_prev = jnp.where(tile > 0, s_prev, -jnp.inf)
                    s_ = jnp.where(upper, s_prev, blk[:, WINDOW:])
                    sink = sinks_ref[h * B_REP + 2 * j + half]
                    m = jnp.maximum(jnp.max(s_, axis=-1, keepdims=True), sink)
                    p = jnp.exp(s_ - m)
                    e_pair.append(jnp.exp(sink - m))
                    p_pair += [jnp.where(upper, p, 0.0), jnp.where(upper, 0.0, p)]
                p_rows.append(jnp.concatenate(p_pair, axis=1).astype(BF16))
                e_rows.append(jnp.where(lane_lo, e_pair[0], e_pair[1]))
            o_all = jnp.dot(jnp.concatenate(p_rows, axis=0), v_bd,
                            preferred_element_type=F32)
            for j in range(spk):
                o = o_all[j * CHUNK:(j + 1) * CHUNK]
                c0 = A_WIDTH + (h * spk + j) * LANES
                y_ref[rows, c0:c0 + LANES] = (o[:, :LANES] / (o[:, LANES:] + e_rows[j])).astype(BF16)
        att += [functools.partial(b_attend, c, h) for c in range(n_chunks) for h in range(B_KV_HEADS)]

        c_scale = C_HEAD_DIM ** -0.5

        def c_head(h):
            cols = slice(h * C_HEAD_DIM, (h + 1) * C_HEAD_DIM)
            qn = _rms(qc_ref[:, cols].astype(F32), gcq_ref[...]).astype(BF16)
            s_ = jnp.dot(qn, kct_ref[0, cols, :], preferred_element_type=F32) * c_scale
            m = jnp.max(s_, axis=-1, keepdims=True)
            p = jnp.exp(s_ - m)
            l = jnp.sum(p, axis=-1, keepdims=True)
            o = jnp.dot(p.astype(BF16), vc_ref[0, :, cols], preferred_element_type=F32) / l
            c0 = A_WIDTH + B_WIDTH + h * C_HEAD_DIM
            y_ref[:, c0:c0 + C_HEAD_DIM] = o.astype(BF16)
        att += [functools.partial(c_head, h) for h in range(C_HEADS)]

        def merge_cols(cb):
            cols = slice(cb * MXU_DIM, (cb + 1) * MXU_DIM)
            za = jnp.dot(yp_ref[:, 0:A_WIDTH], wa_ref[:, cols], preferred_element_type=F32)
            zb = jnp.dot(yp_ref[:, A_WIDTH:A_WIDTH + B_WIDTH], wb_ref[:, cols],
                         preferred_element_type=F32)
            zc = jnp.dot(yp_ref[:, A_WIDTH + B_WIDTH:], wc_ref[:, cols],
                         preferred_element_type=F32)
            mg_ref[:, cols] = (g0_ref[:, cols].astype(F32) * za + g1_ref[:, cols].astype(F32) * zb
                               + g2_ref[:, cols].astype(F32) * zc).astype(BF16)

        def out_cols(cb):
            cols = slice(cb * MXU_DIM, (cb + 1) * MXU_DIM)
            o_ref[:, cols] = x_ref[:, cols] + jnp.dot(mg_ref[...], wo_ref[:, cols],
                                                      preferred_element_type=F32)
        n_cb = D_MODEL // MXU_DIM
        tail += [functools.partial(merge_cols, cb) for cb in range(n_cb)]
        tail += [functools.partial(out_cols, cb) for cb in range(n_cb)]

        n_a, n_t = len(att), len(tail)
        ti = 0
        for ai, piece in enumerate(att):
            piece()
            while ti < n_t and (ti + 1) * n_a <= (ai + 1) * n_t:
                tail[ti]()
                ti += 1
        while ti < n_t:
            tail[ti]()
            ti += 1

    for par, (y_ref, yp_ref) in enumerate(((y0_ref, y1_ref), (y1_ref, y0_ref))):
        @pl.when(step % 2 == par)
        def _():
            body(y_ref, yp_ref)


def _mixers(x2, proj, pos2, kct, vc, sinks, g_a_v, w_sp, bias_sp, gq_t, gk_t, invf, gcq,
            bd, wa, wb, wc, wo, bn, s_len):
    t = x2.shape[0]
    tm = MIX_TM
    tps = s_len // tm
    n_tiles = t // tm

    def cur(s):
        return jnp.minimum(s, n_tiles - 1)

    def prev(s):
        return jnp.maximum(s - 1, 0)

    def const(*idx):
        return lambda s: idx

    in_specs = [
        pl.BlockSpec(memory_space=pltpu.SMEM),
        pl.BlockSpec((tm, D_MODEL), lambda s: (prev(s), 0)),
        pl.BlockSpec((tm, D_MODEL), lambda s: (prev(s), 0)),
        pl.BlockSpec((tm, D_MODEL), lambda s: (prev(s), 1)),
        pl.BlockSpec((tm, D_MODEL), lambda s: (prev(s), 2)),
        pl.BlockSpec((tm, B_WIDTH), lambda s: (cur(s), COL_QB // B_WIDTH)),
        pl.BlockSpec((tm, 2 * A_WIDTH), lambda s: (cur(s), COL_UV // (2 * A_WIDTH))),
        pl.BlockSpec((tm, C_WIDTH), lambda s: (cur(s), COL_QC // C_WIDTH)),
        pl.BlockSpec((tm, 2 * B_KV_WIDTH), lambda s: (cur(s), COL_KV // (2 * B_KV_WIDTH))),
        pl.BlockSpec((tm, LANES), lambda s: (cur(s), 0)),
        pl.BlockSpec((1, C_WIDTH, MEM_LEN), lambda s: (cur(s) // tps, 0, 0)),
        pl.BlockSpec((1, MEM_LEN, C_WIDTH), lambda s: (cur(s) // tps, 0, 0)),
        pl.BlockSpec((1, A_WIDTH), const(0, 0)),
        pl.BlockSpec((A_GROUPS, CHUNK, CHUNK), const(0, 0, 0)),
        pl.BlockSpec((CHUNK, A_WIDTH), const(0, 0)),
        pl.BlockSpec((1, B_WIDTH), const(0, 0)),
        pl.BlockSpec((1, B_KV_WIDTH), const(0, 0)),
        pl.BlockSpec((1, LANES), const(0, 0)),
        pl.BlockSpec((1, C_HEAD_DIM), const(0, 0)),
        pl.BlockSpec((MXU_DIM, MXU_DIM), const(0, 0)),
        pl.BlockSpec((A_WIDTH, D_MODEL), const(0, 0), pipeline_mode=pl.Buffered(1)),
        pl.BlockSpec((B_WIDTH, D_MODEL), const(0, 0), pipeline_mode=pl.Buffered(1)),
        pl.BlockSpec((C_WIDTH, D_MODEL), const(0, 0), pipeline_mode=pl.Buffered(1)),
        pl.BlockSpec((D_MODEL, D_MODEL), const(0, 0), pipeline_mode=pl.Buffered(1)),
    ]
    return pl.pallas_call(
        functools.partial(_mixer_kernel, tps),
        grid=(n_tiles + 1,),
        in_specs=in_specs,
        out_specs=pl.BlockSpec((tm, D_MODEL), lambda s: (prev(s), 0)),
        out_shape=jax.ShapeDtypeStruct((t, D_MODEL), F32),
        scratch_shapes=[
            pltpu.VMEM((tm, D_MODEL), BF16),
            pltpu.VMEM((tm, D_MODEL), BF16),
            pltpu.VMEM((tm, D_MODEL), BF16),
            pltpu.VMEM((WINDOW, B_KV_WIDTH), F32),
            pltpu.VMEM((WINDOW, B_KV_WIDTH), BF16),
        ],
        compiler_params=pltpu.CompilerParams(
            dimension_semantics=("arbitrary",),
            vmem_limit_bytes=VMEM_LIMIT_BYTES),
        name="mixers",
    )(sinks, x2, proj, proj, proj, proj, proj, proj, proj, pos2, kct, vc,
      g_a_v, w_sp, bias_sp, gq_t, gk_t, invf, gcq, bd, wa, wb, wc, wo)


def _ffn_kernel(tiles_per_seq, x_hbm, g_ref, wua_ref, wub_ref, cwa_ref, cwb_ref, cba_ref,
                cbb_ref, wd_ref, o_ref, h_ref, tail_ref, ubuf_ref, xbuf_ref, x_sem):
    i = pl.program_id(0)
    f = pl.program_id(1)
    nf = pl.num_programs(1) - 1
    tm = xbuf_ref.shape[0]
    fc = wd_ref.shape[0]
    ppw = fc // MXU_DIM
    n_up = 2 * ppw

    def x_copy(tile_idx):
        return pltpu.make_async_copy(x_hbm.at[pl.ds(tile_idx * tm, tm), :], xbuf_ref, x_sem)
    n_gate = tm // CHUNK
    gpu = n_gate // n_up

    def up_piece(k, r0=0, nrows=None):
        nrows = tm if nrows is None else nrows
        slot = f % 2
        w_ref = wua_ref if k < ppw else wub_ref
        wcols = slice((k % ppw) * MXU_DIM, (k % ppw + 1) * MXU_DIM)
        cols = slice(k * MXU_DIM, (k + 1) * MXU_DIM)
        up = jnp.dot(h_ref[r0:r0 + nrows, :], w_ref[:, wcols], preferred_element_type=F32)
        if r0 == 0:
            seq_start = i % tiles_per_seq == 0
            ubuf_ref[slot, 0:SUBLANES, cols] = jnp.where(seq_start, 0.0, tail_ref[f, :, cols])
        ubuf_ref[slot, SUBLANES + r0:SUBLANES + r0 + nrows, cols] = up
        if r0 + nrows == tm:
            tail_ref[f, :, cols] = up[nrows - SUBLANES:]

    def gate_piece(k):
        slot = (f + 1) % 2
        r0 = SUBLANES + k * CHUNK
        cw = jnp.concatenate([cwa_ref[...], cwb_ref[...]], axis=1)
        cb = jnp.concatenate([cba_ref[...], cbb_ref[...]], axis=1)
        conv = (cw[0:1] * ubuf_ref[slot, r0 - 2:r0 - 2 + CHUNK, :]
                + cw[1:2] * ubuf_ref[slot, r0 - 1:r0 - 1 + CHUNK, :]
                + cw[2:3] * ubuf_ref[slot, r0:r0 + CHUNK, :] + cb)
        ha = 0.5 * conv[:, :fc]
        return ((ha + ha * jnp.tanh(ha)) * conv[:, fc:]).astype(BF16)

    @pl.when(f == 0)
    def _():
        @pl.when(i == 0)
        def _():
            x_copy(i).start()
            tail_ref[...] = jnp.zeros_like(tail_ref)

        x_copy(i).wait()
        for r0 in range(0, tm, MXU_DIM):
            x = xbuf_ref[r0:r0 + MXU_DIM, :]
            h_ref[r0:r0 + MXU_DIM, :] = _rms(x, g_ref[...]).astype(BF16)
            o_ref[r0:r0 + MXU_DIM, :] = x
            for k in range(n_up):
                up_piece(k, r0, MXU_DIM)

    @pl.when((f == 1) & (i + 1 < pl.num_programs(0)))
    def _():
        x_copy(i + 1).start()

    @pl.when((f > 0) & (f < nf))
    def _():
        acts = []
        for k in range(n_up):
            up_piece(k)
            acts += [gate_piece(gpu * k + g) for g in range(gpu)]
        act = jnp.concatenate(acts, axis=0)
        o_ref[...] += jnp.dot(act, wd_ref[...], preferred_element_type=F32)

    @pl.when(f == nf)
    def _():
        act = jnp.concatenate([gate_piece(k) for k in range(n_gate)], axis=0)
        o_ref[...] += jnp.dot(act, wd_ref[...], preferred_element_type=F32)


def _conv_ffn(x2, g_ffn, w_up, conv_w, conv_b, w_down, s_len):
    t = x2.shape[0]
    tm, fc = FFN_TM, FFN_FC
    nf = D_FF // fc

    def up_idx(f):
        return jnp.minimum(f, nf - 1)

    def down_idx(f):
        return jnp.maximum(f - 1, 0)

    return pl.pallas_call(
        functools.partial(_ffn_kernel, s_len // tm),
        grid=(t // tm, nf + 1),
        in_specs=[
            pl.BlockSpec(memory_space=pl.ANY),
            pl.BlockSpec((1, D_MODEL), lambda i, f: (0, 0)),
            pl.BlockSpec((D_MODEL, fc), lambda i, f: (0, up_idx(f))),
            pl.BlockSpec((D_MODEL, fc), lambda i, f: (0, nf + up_idx(f))),
            pl.BlockSpec((CONV_W, fc), lambda i, f: (0, down_idx(f))),
            pl.BlockSpec((CONV_W, fc), lambda i, f: (0, nf + down_idx(f))),
            pl.BlockSpec((1, fc), lambda i, f: (0, down_idx(f))),
            pl.BlockSpec((1, fc), lambda i, f: (0, nf + down_idx(f))),
            pl.BlockSpec((fc, D_MODEL), lambda i, f: (down_idx(f), 0)),
        ],
        out_specs=pl.BlockSpec((tm, D_MODEL), lambda i, f: (i, 0)),
        out_shape=jax.ShapeDtypeStruct((t, D_MODEL), F32),
        scratch_shapes=[
            pltpu.VMEM((tm, D_MODEL), BF16),
            pltpu.VMEM((nf, SUBLANES, 2 * fc), F32),
            pltpu.VMEM((2, tm + SUBLANES, 2 * fc), F32),
            pltpu.VMEM((tm, D_MODEL), F32),
            pltpu.SemaphoreType.DMA(()),
        ],
        compiler_params=pltpu.CompilerParams(
            dimension_semantics=("arbitrary", "arbitrary"),
            vmem_limit_bytes=VMEM_LIMIT_BYTES),
        name="conv_ffn",
    )(x2, g_ffn, w_up, w_up, conv_w, conv_w, conv_b, conv_b, w_down)


def _in_col_pieces():
    o = np.cumsum([0, A_WIDTH, A_WIDTH, B_WIDTH, B_KV_WIDTH, B_KV_WIDTH, C_WIDTH]).tolist()
    u_a, v_a, q_b, k_b, v_b, q_c = ((o[n], o[n + 1] - o[n]) for n in range(6))
    return [(MAIN_COLS, N_BRANCH * D_MODEL), q_b, u_a, v_a, q_c, k_b, v_b]


def _reorder_cast_kernel(w_ref, o_ref):
    dst = 0
    for src, width in _in_col_pieces():
        o_ref[:, dst:dst + width] = w_ref[:, src:src + width].astype(BF16)
        dst += width


def _reorder_in_cols(w_in):
    rows = 256
    return pl.pallas_call(
        _reorder_cast_kernel,
        grid=(D_MODEL // rows,),
        in_specs=[pl.BlockSpec((rows, IN_COLS), lambda r: (r, 0))],
        out_specs=pl.BlockSpec((rows, IN_COLS), lambda r: (r, 0)),
        out_shape=jax.ShapeDtypeStruct((D_MODEL, IN_COLS), BF16),
        compiler_params=pltpu.CompilerParams(
            dimension_semantics=("arbitrary",), vmem_limit_bytes=VMEM_LIMIT_BYTES),
        name="w_in_cast",
    )(w_in)


def kernel(x, mem, positions, g_mix, w_in, g_a_v, w_spatial, b_spatial, g_b_q, g_b_k, sinks,
           g_mem, w_mem_kv, g_c_q, g_c_k, w_branch_a, w_branch_b, w_branch_c, w_out, g_ffn,
           w_up, conv_w, conv_b, w_down):
    bn, s_len, _ = x.shape
    depth = w_in.shape[0]
    assert s_len % MIX_TM == 0 and s_len % FFN_TM == 0 and (bn * s_len) % IN_TM == 0
    t = bn * s_len
    x2 = x.reshape(t, D_MODEL)
    pos2 = jnp.broadcast_to(positions.astype(F32).reshape(t, 1), (t, LANES))

    inv = ROPE_THETA ** (-jnp.arange(ROPE_HALF, dtype=F32) / ROPE_HALF)
    inv_head = jnp.concatenate([inv, inv, jnp.zeros((B_HEAD_DIM - ROPE_DIM,), F32)])
    invf = jnp.tile(inv_head, LANES // B_HEAD_DIM).reshape(1, LANES)
    grp = np.arange(MXU_DIM) // B_HEAD_DIM
    bd = jnp.asarray(grp[:, None] == grp[None, :], dtype=BF16)

    for l in range(depth):
        kct, vc = _mem_kv(mem, g_mem[l].reshape(1, D_MODEL), w_mem_kv[l].astype(BF16),
                          g_c_k[l].reshape(1, C_HEAD_DIM))
        proj, wb16 = _in_proj(
            x2, g_mix[l].reshape(1, D_MODEL), _reorder_in_cols(w_in[l]),
            dict(w_up=w_up[l], w_down=w_down[l], w_out=w_out[l], w_branch_a=w_branch_a[l],
                 w_branch_b=w_branch_b[l], w_branch_c=w_branch_c[l]))
        bias_sp = jnp.repeat(b_spatial[l].T, A_GROUP_CH, axis=1)
        x2 = _mixers(
            x2, proj, pos2, kct, vc, sinks[l], g_a_v[l].reshape(1, A_WIDTH), w_spatial[l],
            bias_sp, jnp.tile(g_b_q[l], B_HEADS).reshape(1, B_WIDTH),
            jnp.tile(g_b_k[l], B_KV_HEADS).reshape(1, B_KV_WIDTH), invf,
            g_c_q[l].reshape(1, C_HEAD_DIM), bd,
            wb16["w_branch_a"], wb16["w_branch_b"], wb16["w_branch_c"], wb16["w_out"],
            bn, s_len)
        x2 = _conv_ffn(x2, g_ffn[l].reshape(1, D_MODEL), wb16["w_up"], conv_w[l],
                       conv_b[l].reshape(1, 2 * D_FF), wb16["w_down"], s_len)
    return x2.reshape(bn, s_len, D_MODEL)
```

```python
import functools

import jax
import jax.numpy as jnp
import numpy as np
from jax import lax
from jax.experimental import pallas as pl
from jax.experimental.pallas import tpu as pltpu

F32 = jnp.float32
BF16 = jnp.bfloat16

D_MODEL = 2048
MEM_LEN = 256
EPS = 1e-6
CHUNK = 128
A_GROUPS = 4
A_GROUP_CH = 128
A_WIDTH = A_GROUPS * A_GROUP_CH
WINDOW = 128
B_HEADS = 16
B_KV_HEADS = 2
B_REP = B_HEADS // B_KV_HEADS
B_HEAD_DIM = 64
B_WIDTH = B_HEADS * B_HEAD_DIM
B_KV_WIDTH = B_KV_HEADS * B_HEAD_DIM
ROPE_DIM = B_HEAD_DIM // 4
ROPE_HALF = ROPE_DIM // 2
ROPE_THETA = 500000.0
C_HEADS = 4
C_HEAD_DIM = 128
C_WIDTH = C_HEADS * C_HEAD_DIM
N_BRANCH = 3
D_FF = 5632
CONV_W = 3
MAIN_COLS = 2 * A_WIDTH + B_WIDTH + 2 * B_KV_WIDTH + C_WIDTH
IN_COLS = MAIN_COLS + N_BRANCH * D_MODEL

LANES = 128
SUBLANES = 8
MXU_DIM = 256
VMEM_LIMIT_BYTES = 56 * 1024 * 1024

IN_TM = 1024
IN_TN = 1280
MIX_TM = 256
FFN_TM = 1024
FFN_FC = 512

COL_QB = N_BRANCH * D_MODEL
COL_UV = COL_QB + B_WIDTH
COL_QC = COL_UV + 2 * A_WIDTH
COL_KV = COL_QC + C_WIDTH


def _sigmoid(x):
    return 0.5 * jnp.tanh(0.5 * x) + 0.5


def _rms(x, g):
    ms = jnp.mean(x * x, axis=-1, keepdims=True)
    return x * lax.rsqrt(ms + EPS) * g


def _mem_kv_kernel(mem_ref, g_ref, w_ref, gk_ref, kt_ref, v_ref):
    mh = _rms(mem_ref[0], g_ref[...]).astype(BF16)
    kv = jnp.dot(mh, w_ref[...], preferred_element_type=F32)
    for h in range(C_HEADS):
        k = _rms(kv[:, h * C_HEAD_DIM:(h + 1) * C_HEAD_DIM], gk_ref[...])
        kt_ref[0, h * C_HEAD_DIM:(h + 1) * C_HEAD_DIM, :] = k.T.astype(BF16)
    v_ref[0] = kv[:, C_WIDTH:].astype(BF16)


def _mem_kv(mem, g_mem, w_kv, g_c_k):
    bn = mem.shape[0]
    return pl.pallas_call(
        _mem_kv_kernel,
        grid=(bn,),
        in_specs=[
            pl.BlockSpec((1, MEM_LEN, D_MODEL), lambda b: (b, 0, 0)),
            pl.BlockSpec((1, D_MODEL), lambda b: (0, 0)),
            pl.BlockSpec((D_MODEL, 2 * C_WIDTH), lambda b: (0, 0)),
            pl.BlockSpec((1, C_HEAD_DIM), lambda b: (0, 0)),
        ],
        out_specs=[
            pl.BlockSpec((1, C_WIDTH, MEM_LEN), lambda b: (b, 0, 0)),
            pl.BlockSpec((1, MEM_LEN, C_WIDTH), lambda b: (b, 0, 0)),
        ],
        out_shape=[
            jax.ShapeDtypeStruct((bn, C_WIDTH, MEM_LEN), BF16),
            jax.ShapeDtypeStruct((bn, MEM_LEN, C_WIDTH), BF16),
        ],
        compiler_params=pltpu.CompilerParams(
            dimension_semantics=("arbitrary",), vmem_limit_bytes=VMEM_LIMIT_BYTES),
        name="mem_kv",
    )(mem, g_mem, w_kv, g_c_k)


def _cast_plan(n_steps):
    plan = [
        ("w_up", D_MODEL, 2 * D_FF, 32, 0),
        ("w_down", D_FF, D_MODEL, 128, 64),
        ("w_out", D_MODEL, D_MODEL, 64, 0),
        ("w_branch_b", B_WIDTH, D_MODEL, 64, 32),
        ("w_branch_a", A_WIDTH, D_MODEL, 64, 48),
        ("w_branch_c", C_WIDTH, D_MODEL, 64, 56),
    ]
    for _, rows, _, br, start in plan:
        assert rows % br == 0 and start + rows // br <= n_steps
    return plan


def _in_proj_kernel(plan, x_ref, g_ref, w_ref, *refs):
    n_side = len(plan)
    side_in, o_ref, side_out, h_ref = (refs[:n_side], refs[n_side],
                                       refs[n_side + 1:2 * n_side + 1], refs[-1])
    j = pl.program_id(1)
    step = pl.program_id(0) * pl.num_programs(1) + j

    for (_, rows, _, br, start), src, dst in zip(plan, side_in, side_out):
        @pl.when((step >= start) & (step < start + rows // br))
        def _():
            dst[...] = src[...].astype(BF16)

    def tile():
        return jnp.dot(h_ref[...], w_ref[...], preferred_element_type=F32)

    gate_tiles, gate_rem = divmod(COL_QB, IN_TN)
    uv_tile, uv_off = divmod(COL_UV, IN_TN)
    uv_end = COL_QC - (uv_tile + 1) * IN_TN
    assert gate_rem > 0 and uv_tile == gate_tiles + 1 and 0 < uv_end < IN_TN
    assert uv_tile + 2 == IN_COLS // IN_TN and gate_tiles > 1

    @pl.when(j == 0)
    def _():
        for r in range(0, IN_TM, MXU_DIM):
            rows = slice(r, r + MXU_DIM)
            h = _rms(x_ref[rows, :], g_ref[...]).astype(BF16)
            h_ref[rows, :] = h
            o_ref[rows, :] = _sigmoid(
                jnp.dot(h, w_ref[...], preferred_element_type=F32)).astype(o_ref.dtype)

    @pl.when((j > 0) & (j < gate_tiles))
    def _():
        o_ref[...] = _sigmoid(tile()).astype(o_ref.dtype)

    @pl.when(j == gate_tiles)
    def _():
        acc = tile()
        o_ref[:, :gate_rem] = _sigmoid(acc[:, :gate_rem]).astype(o_ref.dtype)
        o_ref[:, gate_rem:] = acc[:, gate_rem:].astype(o_ref.dtype)

    @pl.when(j == uv_tile)
    def _():
        acc = tile()
        o_ref[:, :uv_off] = acc[:, :uv_off].astype(o_ref.dtype)
        o_ref[:, uv_off:] = jax.nn.gelu(acc[:, uv_off:]).astype(o_ref.dtype)

    @pl.when(j == uv_tile + 1)
    def _():
        acc = tile()
        o_ref[:, :uv_end] = jax.nn.gelu(acc[:, :uv_end]).astype(o_ref.dtype)
        o_ref[:, uv_end:] = acc[:, uv_end:].astype(o_ref.dtype)


def _in_proj(x2, g_mix, w_in_r, later_weights):
    t = x2.shape[0]
    ni, nj = t // IN_TM, IN_COLS // IN_TN
    plan = _cast_plan(ni * nj)

    def side_spec(rows, cols, br, start):
        nblk = rows // br
        return pl.BlockSpec((br, cols), lambda i, j: (jnp.clip(i * nj + j - start, 0, nblk - 1), 0))

    side_specs = [side_spec(rows, cols, br, start) for _, rows, cols, br, start in plan]
    outs = pl.pallas_call(
        functools.partial(_in_proj_kernel, plan),
        grid=(ni, nj),
        in_specs=[
            pl.BlockSpec((IN_TM, D_MODEL), lambda i, j: (i, 0)),
            pl.BlockSpec((1, D_MODEL), lambda i, j: (0, 0)),
            pl.BlockSpec((D_MODEL, IN_TN), lambda i, j: (0, j)),
        ] + side_specs,
        out_specs=[pl.BlockSpec((IN_TM, IN_TN), lambda i, j: (i, j))] + side_specs,
        out_shape=[jax.ShapeDtypeStruct((t, IN_COLS), BF16)]
        + [jax.ShapeDtypeStruct((rows, cols), BF16) for _, rows, cols, _, _ in plan],
        scratch_shapes=[pltpu.VMEM((IN_TM, D_MODEL), BF16)],
        compiler_params=pltpu.CompilerParams(
            dimension_semantics=("arbitrary", "arbitrary"),
            vmem_limit_bytes=VMEM_LIMIT_BYTES),
        name="in_proj",
    )(x2, g_mix, w_in_r, *[later_weights[name] for name, *_ in plan])
    return outs[0], {name: o for (name, *_), o in zip(plan, outs[1:])}


def _rope_tables(pos, invf):
    ang = pos * invf
    cos = jnp.cos(ang)
    sin = jnp.sin(ang)
    lane = lax.broadcasted_iota(jnp.int32, ang.shape, 1) % B_HEAD_DIM
    s_lo = jnp.where(lane < ROPE_HALF, -sin, 0.0)
    s_hi = jnp.where(lane >= ROPE_HALF, sin, 0.0)
    return cos, s_lo, s_hi


def _rope(z, tabs):
    cos, s_lo, s_hi = tabs
    return (z * cos + pltpu.roll(z, LANES - ROPE_HALF, 1) * s_lo
            + pltpu.roll(z, ROPE_HALF, 1) * s_hi)


def _head_norm(z, bd, g):
    ssq = jnp.dot((z * z).astype(BF16), bd, preferred_element_type=F32)
    return z * lax.rsqrt(ssq * (1.0 / B_HEAD_DIM) + EPS) * g


def _mixer_kernel(tps, sinks_ref, x_ref, g_ref, qb_ref, uv_ref, qc_ref,
                  kv_ref, pos_ref, kct_ref, vc_ref,
                  gav_ref, wsp_ref, bsp_ref, gq_ref, gk_ref, invf_ref, gcq_ref, bd_ref,
                  wa_ref, wb_ref, wc_ref, wo_ref, o_ref, y0_ref, y1_ref, mg_ref,
                  kprev_ref, vprev_ref):
    step = pl.program_id(0)
    tile = jnp.minimum(step, pl.num_programs(0) - 2) % tps
    tm = x_ref.shape[0]
    n_chunks = tm // CHUNK

    @pl.when(step == 0)
    def _():
        y1_ref[...] = jnp.zeros_like(y1_ref)

    @pl.when(tile == 0)
    def _():
        kprev_ref[...] = jnp.zeros_like(kprev_ref)
        vprev_ref[...] = jnp.zeros_like(vprev_ref)

    def body(y_ref, yp_ref):
        att, tail = [], []
        st = {}
        bd = bd_ref[...]
        bd_kv = bd[:B_KV_WIDTH, :B_KV_WIDTH]
        row = lax.broadcasted_iota(jnp.int32, (CHUNK, CHUNK), 0)
        col = lax.broadcasted_iota(jnp.int32, (CHUNK, CHUNK), 1)

        def a_chunk(c):
            rows = slice(c * CHUNK, (c + 1) * CHUNK)
            u = uv_ref[rows, :A_WIDTH].astype(F32)
            v = _rms(uv_ref[rows, A_WIDTH:].astype(F32), gav_ref[...]).astype(BF16)
            s_ = jnp.concatenate(
                [jnp.dot(jnp.where(row >= col, wsp_ref[g], 0.0).astype(BF16),
                         v[:, g * A_GROUP_CH:(g + 1) * A_GROUP_CH],
                         preferred_element_type=F32) for g in range(A_GROUPS)], axis=1)
            y_ref[rows, 0:A_WIDTH] = (u * (s_ + bsp_ref[...])).astype(BF16)
        att += [functools.partial(a_chunk, c) for c in range(n_chunks)]

        def b_keys():
            tabs = _rope_tables(pos_ref[...], invf_ref[...])
            k_cur = _rope(_head_norm(kv_ref[:, :B_KV_WIDTH].astype(F32), bd_kv, gk_ref[...]), tabs)
            st["tabs"] = tabs
            st["kt_all"] = jnp.concatenate([kprev_ref[...], k_cur], axis=0).T.astype(BF16)
            st["v_all"] = jnp.concatenate([vprev_ref[...], kv_ref[:, B_KV_WIDTH:]], axis=0)
            kprev_ref[...] = k_cur[tm - WINDOW:]
            vprev_ref[...] = kv_ref[tm - WINDOW:, B_KV_WIDTH:]
            st["q"] = []
        att.append(b_keys)

        scale = B_HEAD_DIM ** -0.5

        def b_q(j):
            z = _head_norm(qb_ref[:, j * MXU_DIM:(j + 1) * MXU_DIM].astype(F32), bd,
                           gq_ref[:, j * MXU_DIM:(j + 1) * MXU_DIM])
            for half in range(MXU_DIM // LANES):
                zz = _rope(z[:, half * LANES:(half + 1) * LANES], st["tabs"]) * scale
                st["q"].append(zz.astype(BF16))
        att += [functools.partial(b_q, j) for j in range(B_WIDTH // MXU_DIM)]

        upper = col > row
        lane_lo = col < B_HEAD_DIM
        spk = B_REP // 2

        def b_attend(c, h):
            zk = jnp.zeros((B_HEAD_DIM, 2 * WINDOW), BF16)
            zv = jnp.zeros((2 * WINDOW, B_HEAD_DIM), BF16)
            ov = jnp.ones((2 * WINDOW, B_HEAD_DIM), BF16)
            rows = slice(c * CHUNK, (c + 1) * CHUNK)
            kt = st["kt_all"][h * B_HEAD_DIM:(h + 1) * B_HEAD_DIM, c * CHUNK:c * CHUNK + 2 * WINDOW]
            v2 = st["v_all"][c * CHUNK:c * CHUNK + 2 * WINDOW, h * B_HEAD_DIM:(h + 1) * B_HEAD_DIM]
            k_bd = jnp.concatenate([jnp.concatenate([kt, zk], axis=1),
                                    jnp.concatenate([zk, kt], axis=1)], axis=0)
            v_bd = jnp.concatenate([jnp.concatenate([v2, zv, ov, zv], axis=1),
                                    jnp.concatenate([zv, v2, zv, ov], axis=1)], axis=0)
            q_stack = jnp.concatenate([st["q"][h * spk + j][rows] for j in range(spk)], axis=0)
            s_all = jnp.dot(q_stack, k_bd, preferred_element_type=F32)
            p_rows, e_rows = [], []
            for j in range(spk):
                p_pair, e_pair = [], []
                for half in range(2):
                    blk = s_all[j * CHUNK:(j + 1) * CHUNK,
                                half * 2 * WINDOW:(half + 1) * 2 * WINDOW]
                    s_prev = blk[:, :WINDOW]
                    if c == 0:
                        s_prev = jnp.where(tile > 0, s_prev, -jnp.inf)
                    s_ = jnp.where(upper, s_prev, blk[:, WINDOW:])
                    sink = sinks_ref[h * B_REP + 2 * j + half]
                    m = jnp.maximum(jnp.max(s_, axis=-1, keepdims=True), sink)
                    p = jnp.exp(s_ - m)
                    e_pair.append(jnp.exp(sink - m))
                    p_pair += [jnp.where(upper, p, 0.0), jnp.where(upper, 0.0, p)]
                p_rows.append(jnp.concatenate(p_pair, axis=1).astype(BF16))
                e_rows.append(jnp.where(lane_lo, e_pair[0], e_pair[1]))
            o_all = jnp.dot(jnp.concatenate(p_rows, axis=0), v_bd,
                            preferred_element_type=F32)
            for j in range(spk):
                o = o_all[j * CHUNK:(j + 1) * CHUNK]
                c0 = A_WIDTH + (h * spk + j) * LANES
                y_ref[rows, c0:c0 + LANES] = (o[:, :LANES] / (o[:, LANES:] + e_rows[j])).astype(BF16)
        att += [functools.partial(b_attend, c, h) for c in range(n_chunks) for h in range(B_KV_HEADS)]

        c_scale = C_HEAD_DIM ** -0.5

        def c_head(h):
            cols = slice(h * C_HEAD_DIM, (h + 1) * C_HEAD_DIM)
            qn = _rms(qc_ref[:, cols].astype(F32), gcq_ref[...]).astype(BF16)
            s_ = jnp.dot(qn, kct_ref[0, cols, :], preferred_element_type=F32) * c_scale
            m = jnp.max(s_, axis=-1, keepdims=True)
            p = jnp.exp(s_ - m)
            l = jnp.sum(p, axis=-1, keepdims=True)
            o = jnp.dot(p.astype(BF16), vc_ref[0, :, cols], preferred_element_type=F32) / l
            c0 = A_WIDTH + B_WIDTH + h * C_HEAD_DIM
            y_ref[:, c0:c0 + C_HEAD_DIM] = o.astype(BF16)
        att += [functools.partial(c_head, h) for h in range(C_HEADS)]

        def merge_cols(cb):
            cols = slice(cb * MXU_DIM, (cb + 1) * MXU_DIM)
            za = jnp.dot(yp_ref[:, 0:A_WIDTH], wa_ref[:, cols], preferred_element_type=F32)
            zb = jnp.dot(yp_ref[:, A_WIDTH:A_WIDTH + B_WIDTH], wb_ref[:, cols],
                         preferred_element_type=F32)
            zc = jnp.dot(yp_ref[:, A_WIDTH + B_WIDTH:], wc_ref[:, cols],
                         preferred_element_type=F32)
            g0, g1, g2 = (g_ref[:, br * D_MODEL + cb * MXU_DIM:br * D_MODEL + (cb + 1) * MXU_DIM]
                          .astype(F32) for br in range(N_BRANCH))
            mg_ref[:, cols] = (g0 * za + g1 * zb + g2 * zc).astype(BF16)

        def out_cols(cb):
            cols = slice(cb * MXU_DIM, (cb + 1) * MXU_DIM)
            o_ref[:, cols] = x_ref[:, cols] + jnp.dot(mg_ref[...], wo_ref[:, cols],
                                                      preferred_element_type=F32)
        n_cb = D_MODEL // MXU_DIM
        tail += [functools.partial(merge_cols, cb) for cb in range(n_cb)]
        tail += [functools.partial(out_cols, cb) for cb in range(n_cb)]

        n_a, n_t = len(att), len(tail)
        ti = 0
        for ai, piece in enumerate(att):
            piece()
            while ti < n_t and (ti + 1) * n_a <= (ai + 1) * n_t:
                tail[ti]()
                ti += 1
        while ti < n_t:
            tail[ti]()
            ti += 1

    for par, (y_ref, yp_ref) in enumerate(((y0_ref, y1_ref), (y1_ref, y0_ref))):
        @pl.when(step % 2 == par)
        def _():
            body(y_ref, yp_ref)


def _mixers(x2, proj, pos2, kct, vc, sinks, g_a_v, w_sp, bias_sp, gq_t, gk_t, invf, gcq,
            bd, wa, wb, wc, wo, bn, s_len):
    t = x2.shape[0]
    tm = MIX_TM
    tps = s_len // tm
    n_tiles = t // tm

    def cur(s):
        return jnp.minimum(s, n_tiles - 1)

    def prev(s):
        return jnp.maximum(s - 1, 0)

    def const(*idx):
        return lambda s: idx

    in_specs = [
        pl.BlockSpec(memory_space=pltpu.SMEM),
        pl.BlockSpec((tm, D_MODEL), lambda s: (prev(s), 0)),
        pl.BlockSpec((tm, N_BRANCH * D_MODEL), lambda s: (prev(s), 0)),
        pl.BlockSpec((tm, B_WIDTH), lambda s: (cur(s), COL_QB // B_WIDTH)),
        pl.BlockSpec((tm, 2 * A_WIDTH), lambda s: (cur(s), COL_UV // (2 * A_WIDTH))),
        pl.BlockSpec((tm, C_WIDTH), lambda s: (cur(s), COL_QC // C_WIDTH)),
        pl.BlockSpec((tm, 2 * B_KV_WIDTH), lambda s: (cur(s), COL_KV // (2 * B_KV_WIDTH))),
        pl.BlockSpec((tm, LANES), lambda s: (cur(s), 0)),
        pl.BlockSpec((1, C_WIDTH, MEM_LEN), lambda s: (cur(s) // tps, 0, 0)),
        pl.BlockSpec((1, MEM_LEN, C_WIDTH), lambda s: (cur(s) // tps, 0, 0)),
        pl.BlockSpec((1, A_WIDTH), const(0, 0)),
        pl.BlockSpec((A_GROUPS, CHUNK, CHUNK), const(0, 0, 0)),
        pl.BlockSpec((CHUNK, A_WIDTH), const(0, 0)),
        pl.BlockSpec((1, B_WIDTH), const(0, 0)),
        pl.BlockSpec((1, B_KV_WIDTH), const(0, 0)),
        pl.BlockSpec((1, LANES), const(0, 0)),
        pl.BlockSpec((1, C_HEAD_DIM), const(0, 0)),
        pl.BlockSpec((MXU_DIM, MXU_DIM), const(0, 0)),
        pl.BlockSpec((A_WIDTH, D_MODEL), const(0, 0), pipeline_mode=pl.Buffered(1)),
        pl.BlockSpec((B_WIDTH, D_MODEL), const(0, 0), pipeline_mode=pl.Buffered(1)),
        pl.BlockSpec((C_WIDTH, D_MODEL), const(0, 0), pipeline_mode=pl.Buffered(1)),
        pl.BlockSpec((D_MODEL, D_MODEL), const(0, 0), pipeline_mode=pl.Buffered(1)),
    ]
    return pl.pallas_call(
        functools.partial(_mixer_kernel, tps),
        grid=(n_tiles + 1,),
        in_specs=in_specs,
        out_specs=pl.BlockSpec((tm, D_MODEL), lambda s: (prev(s), 0)),
        out_shape=jax.ShapeDtypeStruct((t, D_MODEL), F32),
        scratch_shapes=[
            pltpu.VMEM((tm, D_MODEL), BF16),
            pltpu.VMEM((tm, D_MODEL), BF16),
            pltpu.VMEM((tm, D_MODEL), BF16),
            pltpu.VMEM((WINDOW, B_KV_WIDTH), F32),
            pltpu.VMEM((WINDOW, B_KV_WIDTH), BF16),
        ],
        compiler_params=pltpu.CompilerParams(
            dimension_semantics=("arbitrary",),
            vmem_limit_bytes=VMEM_LIMIT_BYTES),
        name="mixers",
    )(sinks, x2, proj, proj, proj, proj, proj, pos2, kct, vc,
      g_a_v, w_sp, bias_sp, gq_t, gk_t, invf, gcq, bd, wa, wb, wc, wo)


def _ffn_kernel(tiles_per_seq, x_hbm, g_ref, wua_ref, wub_ref, cwa_ref, cwb_ref, cba_ref,
                cbb_ref, wd_ref, o_ref, h_ref, tail_ref, ubuf_ref, xbuf_ref, x_sem):
    i = pl.program_id(0)
    f = pl.program_id(1)
    nf = pl.num_programs(1) - 1
    tm = xbuf_ref.shape[0]
    fc = wd_ref.shape[0]
    ppw = fc // MXU_DIM
    n_up = 2 * ppw

    def x_copy(tile_idx):
        return pltpu.make_async_copy(x_hbm.at[pl.ds(tile_idx * tm, tm), :], xbuf_ref, x_sem)
    n_gate = tm // CHUNK

    def up_piece(k, r0=0, nrows=None):
        nrows = tm if nrows is None else nrows
        slot = f % 2
        w_ref = wua_ref if k < ppw else wub_ref
        wcols = slice((k % ppw) * MXU_DIM, (k % ppw + 1) * MXU_DIM)
        cols = slice(k * MXU_DIM, (k + 1) * MXU_DIM)
        up = jnp.dot(h_ref[r0:r0 + nrows, :], w_ref[:, wcols], preferred_element_type=F32)
        if r0 == 0:
            seq_start = i % tiles_per_seq == 0
            ubuf_ref[slot, 0:SUBLANES, cols] = jnp.where(seq_start, 0.0, tail_ref[f, :, cols])
        ubuf_ref[slot, SUBLANES + r0:SUBLANES + r0 + nrows, cols] = up
        if r0 + nrows == tm:
            tail_ref[f, :, cols] = up[nrows - SUBLANES:]

    def gate_piece(k):
        slot = (f + 1) % 2
        r0 = SUBLANES + k * CHUNK
        cw = jnp.concatenate([cwa_ref[...], cwb_ref[...]], axis=1)
        cb = jnp.concatenate([cba_ref[...], cbb_ref[...]], axis=1)
        conv = (cw[0:1] * ubuf_ref[slot, r0 - 2:r0 - 2 + CHUNK, :]
                + cw[1:2] * ubuf_ref[slot, r0 - 1:r0 - 1 + CHUNK, :]
                + cw[2:3] * ubuf_ref[slot, r0:r0 + CHUNK, :] + cb)
        ha = 0.5 * conv[:, :fc]
        return ((ha + ha * jnp.tanh(ha)) * conv[:, fc:]).astype(BF16)

    @pl.when(f == 0)
    def _():
        @pl.when(i == 0)
        def _():
            x_copy(i).start()
            tail_ref[...] = jnp.zeros_like(tail_ref)

        x_copy(i).wait()
        for r0 in range(0, tm, MXU_DIM):
            x = xbuf_ref[r0:r0 + MXU_DIM, :]
            h_ref[r0:r0 + MXU_DIM, :] = _rms(x, g_ref[...]).astype(BF16)
            o_ref[r0:r0 + MXU_DIM, :] = x
            for k in range(n_up):
                up_piece(k, r0, MXU_DIM)

    @pl.when((f == 1) & (i + 1 < pl.num_programs(0)))
    def _():
        x_copy(i + 1).start()

    @pl.when((f > 0) & (f < nf))
    def _():
        acts = []
        for k in range(n_up // 2):
            up_piece(k)
        for k in range(n_up // 2, n_up):
            acts += [gate_piece(g) for g in range((k - n_up // 2) * n_gate // 2,
                                                  (k - n_up // 2 + 1) * n_gate // 2)]
            up_piece(k)
        act = jnp.concatenate(acts, axis=0)
        o_ref[...] += jnp.dot(act, wd_ref[...], preferred_element_type=F32)

    @pl.when(f == nf)
    def _():
        act = jnp.concatenate([gate_piece(k) for k in range(n_gate)], axis=0)
        o_ref[...] += jnp.dot(act, wd_ref[...], preferred_element_type=F32)


def _conv_ffn(x2, g_ffn, w_up, conv_w, conv_b, w_down, s_len):
    t = x2.shape[0]
    tm, fc = FFN_TM, FFN_FC
    nf = D_FF // fc

    def up_idx(f):
        return jnp.minimum(f, nf - 1)

    def down_idx(f):
        return jnp.maximum(f - 1, 0)

    return pl.pallas_call(
        functools.partial(_ffn_kernel, s_len // tm),
        grid=(t // tm, nf + 1),
        in_specs=[
            pl.BlockSpec(memory_space=pl.ANY),
            pl.BlockSpec((1, D_MODEL), lambda i, f: (0, 0)),
            pl.BlockSpec((D_MODEL, fc), lambda i, f: (0, up_idx(f))),
            pl.BlockSpec((D_MODEL, fc), lambda i, f: (0, nf + up_idx(f))),
            pl.BlockSpec((CONV_W, fc), lambda i, f: (0, down_idx(f))),
            pl.BlockSpec((CONV_W, fc), lambda i, f: (0, nf + down_idx(f))),
            pl.BlockSpec((1, fc), lambda i, f: (0, down_idx(f))),
            pl.BlockSpec((1, fc), lambda i, f: (0, nf + down_idx(f))),
            pl.BlockSpec((fc, D_MODEL), lambda i, f: (down_idx(f), 0)),
        ],
        out_specs=pl.BlockSpec((tm, D_MODEL), lambda i, f: (i, 0)),
        out_shape=jax.ShapeDtypeStruct((t, D_MODEL), F32),
        scratch_shapes=[
            pltpu.VMEM((tm, D_MODEL), BF16),
            pltpu.VMEM((nf, SUBLANES, 2 * fc), F32),
            pltpu.VMEM((2, tm + SUBLANES, 2 * fc), F32),
            pltpu.VMEM((tm, D_MODEL), F32),
            pltpu.SemaphoreType.DMA(()),
        ],
        compiler_params=pltpu.CompilerParams(
            dimension_semantics=("arbitrary", "arbitrary"),
            vmem_limit_bytes=VMEM_LIMIT_BYTES),
        name="conv_ffn",
    )(x2, g_ffn, w_up, w_up, conv_w, conv_w, conv_b, conv_b, w_down)


def _in_col_pieces():
    o = np.cumsum([0, A_WIDTH, A_WIDTH, B_WIDTH, B_KV_WIDTH, B_KV_WIDTH, C_WIDTH]).tolist()
    u_a, v_a, q_b, k_b, v_b, q_c = ((o[n], o[n + 1] - o[n]) for n in range(6))
    return [(MAIN_COLS, N_BRANCH * D_MODEL), q_b, u_a, v_a, q_c, k_b, v_b]


def _reorder_cast_kernel(w_ref, o_ref):
    dst = 0
    for src, width in _in_col_pieces():
        o_ref[:, dst:dst + width] = w_ref[:, src:src + width].astype(BF16)
        dst += width


def _reorder_in_cols(w_in):
    rows = 256
    return pl.pallas_call(
        _reorder_cast_kernel,
        grid=(D_MODEL // rows,),
        in_specs=[pl.BlockSpec((rows, IN_COLS), lambda r: (r, 0))],
        out_specs=pl.BlockSpec((rows, IN_COLS), lambda r: (r, 0)),
        out_shape=jax.ShapeDtypeStruct((D_MODEL, IN_COLS), BF16),
        compiler_params=pltpu.CompilerParams(
            dimension_semantics=("arbitrary",), vmem_limit_bytes=VMEM_LIMIT_BYTES),
        name="w_in_cast",
    )(w_in)


def kernel(x, mem, positions, g_mix, w_in, g_a_v, w_spatial, b_spatial, g_b_q, g_b_k, sinks,
           g_mem, w_mem_kv, g_c_q, g_c_k, w_branch_a, w_branch_b, w_branch_c, w_out, g_ffn,
           w_up, conv_w, conv_b, w_down):
    bn, s_len, _ = x.shape
    depth = w_in.shape[0]
    assert s_len % MIX_TM == 0 and s_len % FFN_TM == 0 and (bn * s_len) % IN_TM == 0
    t = bn * s_len
    x2 = x.reshape(t, D_MODEL)
    pos2 = jnp.broadcast_to(positions.astype(F32).reshape(t, 1), (t, LANES))

    inv = ROPE_THETA ** (-jnp.arange(ROPE_HALF, dtype=F32) / ROPE_HALF)
    inv_head = jnp.concatenate([inv, inv, jnp.zeros((B_HEAD_DIM - ROPE_DIM,), F32)])
    invf = jnp.tile(inv_head, LANES // B_HEAD_DIM).reshape(1, LANES)
    grp = np.arange(MXU_DIM) // B_HEAD_DIM
    bd = jnp.asarray(grp[:, None] == grp[None, :], dtype=BF16)

    for l in range(depth):
        kct, vc = _mem_kv(mem, g_mem[l].reshape(1, D_MODEL), w_mem_kv[l].astype(BF16),
                          g_c_k[l].reshape(1, C_HEAD_DIM))
        proj, wb16 = _in_proj(
            x2, g_mix[l].reshape(1, D_MODEL), _reorder_in_cols(w_in[l]),
            dict(w_up=w_up[l], w_down=w_down[l], w_out=w_out[l], w_branch_a=w_branch_a[l],
                 w_branch_b=w_branch_b[l], w_branch_c=w_branch_c[l]))
        bias_sp = jnp.repeat(b_spatial[l].T, A_GROUP_CH, axis=1)
        x2 = _mixers(
            x2, proj, pos2, kct, vc, sinks[l], g_a_v[l].reshape(1, A_WIDTH), w_spatial[l],
            bias_sp, jnp.tile(g_b_q[l], B_HEADS).reshape(1, B_WIDTH),
            jnp.tile(g_b_k[l], B_KV_HEADS).reshape(1, B_KV_WIDTH), invf,
            g_c_q[l].reshape(1, C_HEAD_DIM), bd,
            wb16["w_branch_a"], wb16["w_branch_b"], wb16["w_branch_c"], wb16["w_out"],
            bn, s_len)
        x2 = _conv_ffn(x2, g_ffn[l].reshape(1, D_MODEL), wb16["w_up"], conv_w[l],
                       conv_b[l].reshape(1, 2 * D_FF), wb16["w_down"], s_len)
    return x2.reshape(bn, s_len, D_MODEL)
```
